```python
import jax, jax.numpy as jnp
from jax import lax
import numpy as np

D_MODEL = 1024
BATCH = 8
SEQ = 4096
DEPTH = 2

HEAD_DIM = 64
ROPE_THETA = 10000.0
Q_BLOCK = 128
MLA_HEADS = 4
MLA_NOPE = 64
MLA_ROPE = 32
MLA_V = 64
MLA_Q_LORA = 256
MLA_KV_LORA = 128
MOBA_HEADS = 6
MOBA_BLOCK = 256
MOBA_TOPK = 3
MOBA_Q_CHUNK = 32
DIL_HEADS = 6
DIL_PAIRS = ((128, 1), (512, 4), (2048, 16))
MLA_W = MLA_HEADS * MLA_V
MOBA_W = MOBA_HEADS * HEAD_DIM
DIL_W = DIL_HEADS * HEAD_DIM
MIX_W = MLA_W + MOBA_W + DIL_W
SPLITS = (MLA_Q_LORA, MLA_KV_LORA, MLA_ROPE, 3 * MOBA_W, 3 * DIL_W)
N_IN = sum(SPLITS)
N_EXPERTS = 64
N_GROUPS = 8
EXP_PER_GROUP = N_EXPERTS // N_GROUPS
TOP_K = 2
D_EXPERT = 256
MOE_BLOCK = 128
ALPHA = (2 * DEPTH) ** 0.25
BETA = (8 * DEPTH) ** -0.25
LN_EPS = 1e-5
RMS_EPS = 1e-6

kernel_name = "hymba_mla_moba_dilated_grouped_moe"


def layer_norm(x, g, b):
    xf = x.astype(jnp.float32)
    mu = xf.mean(-1, keepdims=True)
    var = jnp.square(xf - mu).mean(-1, keepdims=True)
    return ((xf - mu) * lax.rsqrt(var + LN_EPS) * g + b).astype(x.dtype)


def rms_norm(x, g):
    xf = x.astype(jnp.float32)
    return (xf * lax.rsqrt(jnp.square(xf).mean(-1, keepdims=True) + RMS_EPS) * g).astype(x.dtype)


def rope_tables(n_pos, dim):
    inv = ROPE_THETA ** (-jnp.arange(0, dim, 2, dtype=jnp.float32) / dim)
    ang = jnp.arange(n_pos, dtype=jnp.float32)[:, None] * inv[None, :]
    return jnp.cos(ang), jnp.sin(ang)


def apply_rope(x, cos, sin):
    x1, x2 = jnp.split(x.astype(jnp.float32), 2, axis=-1)
    return jnp.concatenate([x1 * cos - x2 * sin, x2 * cos + x1 * sin], axis=-1).astype(x.dtype)


def causal_dense_attention(q, k, v):
    B, H, S, Dk = q.shape
    scale = Dk ** -0.5
    kpos = jnp.arange(S)

    def one_block(i):
        qb = lax.dynamic_slice_in_dim(q, i * Q_BLOCK, Q_BLOCK, axis=2)
        s = jnp.einsum('bhqd,bhkd->bhqk', qb, k, preferred_element_type=jnp.float32) * scale
        qpos = i * Q_BLOCK + jnp.arange(Q_BLOCK)
        s = jnp.where(kpos[None, :] <= qpos[:, None], s, -jnp.inf)
        p = jax.nn.softmax(s, axis=-1)
        return jnp.einsum('bhqk,bhkd->bhqd', p.astype(v.dtype), v)

    o = lax.map(one_block, jnp.arange(S // Q_BLOCK))
    return o.transpose(1, 2, 0, 3, 4).reshape(B, H, S, v.shape[-1])


def mla_attention(c_q, c_kv, k_r, q_norm, w_uq, kv_norm, w_ukv):
    B, S, _ = c_q.shape
    cos, sin = rope_tables(S, MLA_ROPE)
    q = (rms_norm(c_q, q_norm) @ w_uq).reshape(B, S, MLA_HEADS, MLA_NOPE + MLA_ROPE).transpose(0, 2, 1, 3)
    q = jnp.concatenate([q[..., :MLA_NOPE], apply_rope(q[..., MLA_NOPE:], cos, sin)], axis=-1)
    kv = (rms_norm(c_kv, kv_norm) @ w_ukv).reshape(B, S, MLA_HEADS, MLA_NOPE + MLA_V).transpose(0, 2, 1, 3)
    k_rope = apply_rope(k_r[:, None], cos, sin)
    k = jnp.concatenate([kv[..., :MLA_NOPE], jnp.broadcast_to(k_rope, (B, MLA_HEADS, S, MLA_ROPE))], axis=-1)
    v = kv[..., MLA_NOPE:]
    return causal_dense_attention(q, k, v)


def moba_attention(q, k, v):
    B, H, S, hd = q.shape
    n_kb = -(-S // MOBA_BLOCK)
    pad = ((0, 0), (0, 0), (0, n_kb * MOBA_BLOCK - S), (0, 0))
    kb = jnp.pad(k, pad).reshape(B, H, n_kb, MOBA_BLOCK, hd)
    vb = jnp.pad(v, pad).reshape(B, H, n_kb, MOBA_BLOCK, hd)
    k_mean = kb.astype(jnp.float32).mean(axis=3)
    gate = jnp.einsum('bhsd,bhnd->bhsn', q.astype(jnp.float32), k_mean)
    q_blk = jnp.arange(S) // MOBA_BLOCK
    gate = jnp.where(jnp.arange(n_kb)[None, :] < q_blk[:, None], gate, -jnp.inf)
    n_sel = min(MOBA_TOPK, n_kb)
    _, g_idx = lax.top_k(gate, n_sel)
    slot_ok = jnp.arange(n_sel)[None, :] < q_blk[:, None]
    bi = jnp.arange(B)[:, None, None, None]
    hi = jnp.arange(H)[None, :, None, None]
    scale = hd ** -0.5
    n_sk = n_sel * MOBA_BLOCK

    def one_chunk(c):
        s0 = c * MOBA_Q_CHUNK
        qc = lax.dynamic_slice_in_dim(q, s0, MOBA_Q_CHUNK, axis=2)
        idx = lax.dynamic_slice_in_dim(g_idx, s0, MOBA_Q_CHUNK, axis=2)
        ok = lax.dynamic_slice_in_dim(slot_ok, s0, MOBA_Q_CHUNK, axis=0)
        ks = kb[bi, hi, idx]
        vs = vb[bi, hi, idx]
        s_sel = jnp.einsum('bhqd,bhqnkd->bhqnk', qc, ks, preferred_element_type=jnp.float32) * scale
        s_sel = jnp.where(ok[None, None, :, :, None], s_sel, -jnp.inf).reshape(B, H, MOBA_Q_CHUNK, n_sk)
        own = s0 // MOBA_BLOCK
        k_own = lax.dynamic_slice_in_dim(kb, own, 1, axis=2)[:, :, 0]
        v_own = lax.dynamic_slice_in_dim(vb, own, 1, axis=2)[:, :, 0]
        s_own = jnp.einsum('bhqd,bhkd->bhqk', qc, k_own, preferred_element_type=jnp.float32) * scale
        qpos = s0 + jnp.arange(MOBA_Q_CHUNK)
        kpos = own * MOBA_BLOCK + jnp.arange(MOBA_BLOCK)
        s_own = jnp.where(kpos[None, :] <= qpos[:, None], s_own, -jnp.inf)
        p = jax.nn.softmax(jnp.concatenate([s_sel, s_own], axis=-1), axis=-1).astype(v.dtype)
        p_sel = p[..., :n_sk].reshape(B, H, MOBA_Q_CHUNK, n_sel, MOBA_BLOCK)
        return (jnp.einsum('bhqnk,bhqnkd->bhqd', p_sel, vs)
                + jnp.einsum('bhqk,bhkd->bhqd', p[..., n_sk:], v_own))

    o = lax.map(one_chunk, jnp.arange(S // MOBA_Q_CHUNK))
    return o.transpose(1, 2, 0, 3, 4).reshape(B, H, S, hd)


def dilated_attention(q, k, v):
    B, H, S, hd = q.shape
    scale = hd ** -0.5
    outs, lses = [], []
    for window, dil in DIL_PAIRS:
        W = window // dil
        unit = dil * W
        S_pad = -(-S // unit) * unit
        L = S_pad // dil
        nb = L // W

        def to_sub(t):
            t = jnp.pad(t, ((0, 0), (0, 0), (0, S_pad - S), (0, 0)))
            return t.reshape(B, H, L, dil, hd).transpose(0, 1, 3, 2, 4)

        def band(t):
            tp = jnp.pad(to_sub(t), ((0, 0), (0, 0), (0, 0), (W, 0), (0, 0))).reshape(B, H, dil, nb + 1, W, hd)
            return jnp.concatenate([tp[:, :, :, :-1], tp[:, :, :, 1:]], axis=4)

        qb = to_sub(q).reshape(B, H, dil, nb, W, hd)
        kw, vw = band(k), band(v)
        s = jnp.einsum('bhrnqd,bhrnkd->bhrnqk', qb, kw, preferred_element_type=jnp.float32) * scale
        ki = jnp.arange(2 * W) - W
        delta = jnp.arange(W)[:, None] - ki[None, :]
        key_abs = jnp.arange(nb)[:, None, None] * W + ki[None, None, :]
        valid = ((delta >= 0) & (delta <= W))[None] & (key_abs >= 0)
        s = jnp.where(valid[None, None, None], s, -jnp.inf)
        m = s.max(axis=-1, keepdims=True)
        e = jnp.exp(s - m)
        l = e.sum(axis=-1, keepdims=True)
        o = jnp.einsum('bhrnqk,bhrnkd->bhrnqd', e, vw.astype(jnp.float32)) / l
        lse = m + jnp.log(l)

        def back(t):
            return t.reshape(B, H, dil, L, t.shape[-1]).transpose(0, 1, 3, 2, 4).reshape(B, H, S_pad, t.shape[-1])[:, :, :S]

        outs.append(back(o))
        lses.append(back(lse))
    w = jax.nn.softmax(jnp.stack(lses, axis=0), axis=0)
    return (w * jnp.stack(outs, axis=0)).sum(axis=0).astype(q.dtype)


def hybrid_mixer(x, w_in, q_norm, w_uq, kv_norm, w_ukv, w_out):
    B, S, _ = x.shape
    h = x @ w_in
    offs = np.cumsum(SPLITS)[:-1].tolist()
    c_q, c_kv, k_r, qkv_b, qkv_c = jnp.split(h, offs, axis=-1)
    cos, sin = rope_tables(S, HEAD_DIM)

    def heads(t, n):
        return t.reshape(B, S, n, HEAD_DIM).transpose(0, 2, 1, 3)

    qb, kb, vb = [heads(t, MOBA_HEADS) for t in jnp.split(qkv_b, 3, axis=-1)]
    qc, kc, vc = [heads(t, DIL_HEADS) for t in jnp.split(qkv_c, 3, axis=-1)]
    o_a = mla_attention(c_q, c_kv, k_r, q_norm, w_uq, kv_norm, w_ukv)
    o_b = moba_attention(apply_rope(qb, cos, sin), apply_rope(kb, cos, sin), vb)
    o_c = dilated_attention(apply_rope(qc, cos, sin), apply_rope(kc, cos, sin), vc)
    o = jnp.concatenate([o_a, o_b, o_c], axis=1).transpose(0, 2, 1, 3).reshape(B, S, MIX_W)
    return o @ w_out


def route(xf, router_w, router_b):
    T = xf.shape[0]
    s = jax.nn.sigmoid((xf @ router_w).astype(jnp.float32))
    sb = (s + router_b.astype(jnp.float32)).reshape(T, N_GROUPS, EXP_PER_GROUP)
    g_sel = jnp.argmax(lax.top_k(sb, 2)[0].sum(-1), axis=-1)
    in_grp = sb[jnp.arange(T), g_sel]
    _, j = lax.top_k(in_grp, TOP_K)
    e_idx = g_sel[:, None] * EXP_PER_GROUP + j
    s_sel = jnp.take_along_axis(s, e_idx, axis=1)
    return e_idx, s_sel / s_sel.sum(-1, keepdims=True)


def moe_ffn(x, router_w, router_b, w1, w3, w2):
    B, S, D = x.shape
    T = B * S
    xf = x.reshape(T, D)
    e_idx, gates = route(xf, router_w, router_b)
    A = T * TOP_K
    e_flat = e_idx.reshape(A)
    t_flat = jnp.repeat(jnp.arange(T), TOP_K)
    order = jnp.argsort(e_flat)
    e_s, t_s, g_s = e_flat[order], t_flat[order], gates.reshape(A)[order]
    counts = jnp.bincount(e_flat, length=N_EXPERTS)
    start = jnp.cumsum(counts) - counts
    padded = (counts + MOE_BLOCK - 1) // MOE_BLOCK * MOE_BLOCK
    pend = jnp.cumsum(padded)
    dest = (pend - padded)[e_s] + (jnp.arange(A) - start[e_s])
    n_rows = A + N_EXPERTS * MOE_BLOCK
    n_blk = n_rows // MOE_BLOCK
    row_tok = jnp.full((n_rows,), T, jnp.int32).at[dest].set(t_s.astype(jnp.int32))
    row_gate = jnp.zeros((n_rows,), x.dtype).at[dest].set(g_s.astype(x.dtype))
    blk_exp = jnp.minimum(jnp.searchsorted(pend, jnp.arange(n_blk) * MOE_BLOCK, side='right'), N_EXPERTS - 1)
    x_pad = jnp.concatenate([xf, jnp.zeros((1, D), x.dtype)], axis=0)

    def one_block(args):
        toks, e = args
        xb = x_pad[toks]
        hb = jax.nn.silu(xb @ w1[e]) * (xb @ w3[e])
        return hb @ w2[e]

    y_rows = lax.map(one_block, (row_tok.reshape(n_blk, MOE_BLOCK), blk_exp)).reshape(n_rows, D)
    y = jax.ops.segment_sum(y_rows * row_gate[:, None], row_tok, num_segments=T + 1)[:T]
    return y.reshape(B, S, D)


def setup_inputs(seed: int = 0) -> dict:
    key = jax.random.key(seed)
    ks = jax.random.split(key, 17)

    def nrm(k, shape, scale):
        return jax.random.normal(k, shape, jnp.float32) * scale

    x = nrm(ks[0], (BATCH, SEQ, D_MODEL), 1.0)
    col = jnp.concatenate([
        jnp.ones((MLA_Q_LORA + MLA_KV_LORA + MLA_ROPE + 2 * MOBA_W,), jnp.float32),
        jnp.full((MOBA_W,), BETA, jnp.float32),
        jnp.ones((2 * DIL_W,), jnp.float32),
        jnp.full((DIL_W,), BETA, jnp.float32)])
    w_in = nrm(ks[1], (DEPTH, D_MODEL, N_IN), D_MODEL ** -0.5) * col
    mla_q_norm = 1.0 + nrm(ks[2], (DEPTH, MLA_Q_LORA), 0.02)
    mla_w_uq = nrm(ks[3], (DEPTH, MLA_Q_LORA, MLA_HEADS * (MLA_NOPE + MLA_ROPE)), MLA_Q_LORA ** -0.5)
    mla_kv_norm = 1.0 + nrm(ks[4], (DEPTH, MLA_KV_LORA), 0.02)
    kv_col = jnp.tile(jnp.concatenate([jnp.ones((MLA_NOPE,), jnp.float32), jnp.full((MLA_V,), BETA, jnp.float32)]), MLA_HEADS)
    mla_w_ukv = nrm(ks[5], (DEPTH, MLA_KV_LORA, MLA_HEADS * (MLA_NOPE + MLA_V)), MLA_KV_LORA ** -0.5) * kv_col
    w_out = nrm(ks[6], (DEPTH, MIX_W, D_MODEL), MIX_W ** -0.5 * BETA)
    ln1_g = 1.0 + nrm(ks[7], (DEPTH, D_MODEL), 0.02)
    ln1_b = nrm(ks[8], (DEPTH, D_MODEL), 0.02)
    router_w = nrm(ks[9], (D_MODEL, N_EXPERTS), D_MODEL ** -0.5)
    router_b = nrm(ks[10], (N_EXPERTS,), 0.01)
    moe_w1 = nrm(ks[11], (DEPTH, N_EXPERTS, D_MODEL, D_EXPERT), D_MODEL ** -0.5)
    moe_w3 = nrm(ks[12], (DEPTH, N_EXPERTS, D_MODEL, D_EXPERT), D_MODEL ** -0.5)
    moe_w2 = nrm(ks[13], (DEPTH, N_EXPERTS, D_EXPERT, D_MODEL), D_EXPERT ** -0.5 * BETA)
    ln2_g = 1.0 + nrm(ks[14], (DEPTH, D_MODEL), 0.02)
    ln2_b = nrm(ks[15], (DEPTH, D_MODEL), 0.02)
    return {"x": x, "w_in": w_in, "mla_q_norm": mla_q_norm, "mla_w_uq": mla_w_uq,
            "mla_kv_norm": mla_kv_norm, "mla_w_ukv": mla_w_ukv, "w_out": w_out,
            "ln1_g": ln1_g, "ln1_b": ln1_b, "router_w": router_w, "router_b": router_b,
            "moe_w1": moe_w1, "moe_w3": moe_w3, "moe_w2": moe_w2,
            "ln2_g": ln2_g, "ln2_b": ln2_b}


def reference(x, w_in, mla_q_norm, mla_w_uq, mla_kv_norm, mla_w_ukv, w_out, ln1_g, ln1_b,
              router_w, router_b, moe_w1, moe_w3, moe_w2, ln2_g, ln2_b):
    for l in range(DEPTH):
        mix = hybrid_mixer(x, w_in[l], mla_q_norm[l], mla_w_uq[l], mla_kv_norm[l], mla_w_ukv[l], w_out[l])
        x = layer_norm(ALPHA * x + mix, ln1_g[l], ln1_b[l])
        ffn = moe_ffn(x, router_w, router_b, moe_w1[l], moe_w3[l], moe_w2[l])
        x = layer_norm(ALPHA * x + ffn, ln2_g[l], ln2_b[l])
    return x
```

```python
import functools

import jax
import jax.numpy as jnp
import numpy as np
from jax import lax
from jax.experimental import pallas as pl
from jax.experimental.pallas import tpu as pltpu

F32 = jnp.float32
BF16 = jnp.bfloat16
LANES = 128
NEG_INF = float("-inf")
VMEM_LIMIT_BYTES = 56 * 1024 * 1024

D_MODEL = 1024
DEPTH = 2
HEAD_DIM = 64
ROPE_THETA = 10000.0
MLA_HEADS = 4
MLA_NOPE = 64
MLA_ROPE = 32
MLA_V = 64
MLA_Q_LORA = 256
MLA_KV_LORA = 128
MOBA_HEADS = 6
MOBA_BLOCK = 256
MOBA_TOPK = 3
DIL_HEADS = 6
DIL_PAIRS = ((128, 1), (512, 4), (2048, 16))
MLA_W = MLA_HEADS * MLA_V
MOBA_W = MOBA_HEADS * HEAD_DIM
DIL_W = DIL_HEADS * HEAD_DIM
N_EXPERTS = 64
N_GROUPS = 8
EXP_PER_GROUP = N_EXPERTS // N_GROUPS
TOP_K = 2
D_EXPERT = 256
MOE_BLOCK = 128
ALPHA = (2 * DEPTH) ** 0.25
LN_EPS = 1e-5
RMS_EPS = 1e-6

SEG_A = 512
SEG_B = 3 * MOBA_W
SEG_C = 3 * DIL_W
N_IN_PAD = SEG_A + SEG_B + SEG_C
MLA_HEAD_PAD = 128
MLA_ROPE_LANE = 64

TM_PROJ = 512
TQ = 256
DIL_W_UNITS = 128


def _cparams(*sem):
    return pltpu.CompilerParams(dimension_semantics=sem, vmem_limit_bytes=VMEM_LIMIT_BYTES)


def _rope(x, cos, sin_signed, half):
    lane = lax.broadcasted_iota(jnp.int32, x.shape, 1)
    first = ((lane // half) % 2) == 0
    rot = jnp.where(first, pltpu.roll(x, LANES - half, 1), pltpu.roll(x, half, 1))
    return x * cos + rot * sin_signed


def _rms(x, g):
    return x * lax.rsqrt(jnp.mean(jnp.square(x), axis=-1, keepdims=True) + RMS_EPS) * g


def _inproj_body(seq_blocks, x_ref, w_ref, qn_ref, kvn_ref, wuq_ref, wukv_ref,
                 cos64_ref, sin64_ref, cosm_ref, sinm_ref,
                 qm_ref, km_ref, vm_ref, qb_ref, kb_ref, vb_ref, qc_ref, kc_ref, vc_ref, sel_ref,
                 kmean_ref):
    i = pl.program_id(0)
    tm = x_ref.shape[0]
    n_kb = kmean_ref.shape[0]
    sblk = i % seq_blocks

    @pl.when(sblk == 0)
    def _():
        kmean_ref[...] = jnp.zeros_like(kmean_ref)

    xb = x_ref[...].astype(BF16)
    cos64, sin64 = cos64_ref[...], sin64_ref[...]
    cosm, sinm = cosm_ref[...], sinm_ref[...]

    h_a = jnp.dot(xb, w_ref[:, 0:SEG_A], preferred_element_type=F32)
    cq = _rms(h_a[:, 0:MLA_Q_LORA], qn_ref[...]).astype(BF16)
    ckv = _rms(h_a[:, MLA_Q_LORA:MLA_Q_LORA + MLA_KV_LORA], kvn_ref[...]).astype(BF16)
    kr = _rope(h_a[:, MLA_Q_LORA + MLA_KV_LORA:SEG_A], cosm, sinm, MLA_ROPE // 2)
    q_m = jnp.dot(cq, wuq_ref[...], preferred_element_type=F32)
    kv_m = jnp.dot(ckv, wukv_ref[...], preferred_element_type=F32)
    mla_scale = (MLA_NOPE + MLA_ROPE) ** -0.5
    for h in range(MLA_HEADS):
        sl = slice(h * MLA_HEAD_PAD, (h + 1) * MLA_HEAD_PAD)
        qm_ref[:, sl] = (_rope(q_m[:, sl], cosm, sinm, MLA_ROPE // 2) * mla_scale).astype(BF16)
        km_ref[:, sl] = (kv_m[:, sl] + kr).astype(BF16)
    vm_ref[...] = kv_m[:, MLA_HEADS * MLA_HEAD_PAD:].astype(BF16)

    h_b = jnp.dot(xb, w_ref[:, SEG_A:SEG_A + SEG_B], preferred_element_type=F32)
    scale = HEAD_DIM ** -0.5
    lane = lax.broadcasted_iota(jnp.int32, (n_kb, LANES), 1)
    pos = sblk * tm + lax.broadcasted_iota(jnp.int32, (1, tm), 1)
    q_blk = pos // MOBA_BLOCK
    jio = lax.broadcasted_iota(jnp.int32, (n_kb, tm), 0)
    elig = jio < q_blk
    sel_rows = []
    for c in range(MOBA_W // LANES):
        sl = slice(c * LANES, (c + 1) * LANES)
        q = _rope(h_b[:, sl], cos64, sin64, HEAD_DIM // 2)
        k = _rope(h_b[:, MOBA_W + c * LANES:MOBA_W + (c + 1) * LANES], cos64, sin64, HEAD_DIM // 2)
        qb_ref[:, sl] = (q * scale).astype(BF16)
        kb_ref[:, sl] = k.astype(BF16)
        km = kmean_ref[:, sl]
        kb_row = lax.broadcasted_iota(jnp.int32, (n_kb, LANES), 0)
        for b in range(tm // MOBA_BLOCK):
            mean_b = jnp.mean(k[b * MOBA_BLOCK:(b + 1) * MOBA_BLOCK], axis=0, keepdims=True)
            km = jnp.where(kb_row == sblk * (tm // MOBA_BLOCK) + b, mean_b, km)
        kmean_ref[:, sl] = km
        for half in range(2):
            in_head = (lane // HEAD_DIM) == half
            gate = lax.dot_general(jnp.where(in_head, km, 0.0), q, (((1,), (1,)), ((), ())),
                                   precision=lax.Precision.HIGHEST,
                                   preferred_element_type=F32)
            gate = jnp.where(elig, gate, NEG_INF)
            for j in range(n_kb):
                gj = gate[j:j + 1, :]
                beats = (gate > gj) | ((gate == gj) & (jio < j))
                cnt = jnp.sum(beats.astype(F32), axis=0, keepdims=True)
                sel_rows.append(jnp.where((cnt < MOBA_TOPK) & elig[j:j + 1, :], 1.0, 0.0))
    sel_rows.append(jnp.zeros((LANES - len(sel_rows), tm), F32))
    sel_ref[...] = jnp.concatenate(sel_rows, axis=0).T
    vb_ref[...] = h_b[:, 2 * MOBA_W:].astype(BF16)

    h_c = jnp.dot(xb, w_ref[:, SEG_A + SEG_B:], preferred_element_type=F32)
    for c in range(DIL_W // LANES):
        sl = slice(c * LANES, (c + 1) * LANES)
        qc_ref[:, sl] = _rope(h_c[:, sl], cos64, sin64, HEAD_DIM // 2) * scale
        kc_ref[:, sl] = _rope(h_c[:, DIL_W + c * LANES:DIL_W + (c + 1) * LANES], cos64, sin64, HEAD_DIM // 2)
    vc_ref[...] = h_c[:, 2 * DIL_W:]


def _inproj(x2, w_in_p, qn, kvn, wuq_p, wukv_p, tabs, seq):
    t = x2.shape[0]
    tm = TM_PROJ
    seq_blocks = seq // tm
    n_kb = seq // MOBA_BLOCK
    row = lambda w: pl.BlockSpec((tm, w), lambda i: (i, 0))
    full = lambda a: pl.BlockSpec(a.shape, lambda i: (0,) * a.ndim)
    tab = pl.BlockSpec((tm, LANES), lambda i: (i % seq_blocks, 0))
    out_shapes = [
        jax.ShapeDtypeStruct((t, MLA_HEADS * MLA_HEAD_PAD), BF16),
        jax.ShapeDtypeStruct((t, MLA_HEADS * MLA_HEAD_PAD), BF16),
        jax.ShapeDtypeStruct((t, MLA_W), BF16),
        jax.ShapeDtypeStruct((t, MOBA_W), BF16),
        jax.ShapeDtypeStruct((t, MOBA_W), BF16),
        jax.ShapeDtypeStruct((t, MOBA_W), BF16),
        jax.ShapeDtypeStruct((t, DIL_W), F32),
        jax.ShapeDtypeStruct((t, DIL_W), F32),
        jax.ShapeDtypeStruct((t, DIL_W), F32),
        jax.ShapeDtypeStruct((t, LANES), F32),
    ]
    return pl.pallas_call(
        functools.partial(_inproj_body, seq_blocks),
        grid=(t // tm,),
        in_specs=[row(D_MODEL), full(w_in_p), full(qn), full(kvn), full(wuq_p), full(wukv_p),
                  tab, tab, tab, tab],
        out_specs=[row(s.shape[1]) for s in out_shapes],
        out_shape=out_shapes,
        scratch_shapes=[pltpu.VMEM((n_kb, MOBA_W), F32)],
        compiler_params=_cparams("arbitrary"),
        name="inproj",
    )(x2, w_in_p, qn, kvn, wuq_p, wukv_p, *tabs)


def _flash_step(q, k, v, carry, mask):
    m, l, acc = carry
    s = lax.dot_general(q, k, (((1,), (1,)), ((), ())), preferred_element_type=F32)
    if mask is not None:
        s = jnp.where(mask, s, NEG_INF)
    m_new = jnp.maximum(m, jnp.max(s, axis=1, keepdims=True))
    alpha = jnp.exp(m - m_new)
    p = jnp.exp(s - m_new)
    l = alpha * l + jnp.sum(p, axis=1, keepdims=True)
    acc = alpha * acc + jnp.dot(p.astype(BF16), v, preferred_element_type=F32)
    return m_new, l, acc


def _flash_init(tq):
    return (jnp.full((tq, 1), NEG_INF, F32), jnp.zeros((tq, 1), F32), jnp.zeros((tq, LANES), F32))


def _causal_mask(tq):
    return (lax.broadcasted_iota(jnp.int32, (tq, tq), 1) <= lax.broadcasted_iota(jnp.int32, (tq, tq), 0))


def _mla_body(q_ref, k_ref, v_ref, o_ref):
    i = pl.program_id(2)
    tq = q_ref.shape[0]
    lane = lax.broadcasted_iota(jnp.int32, (tq, LANES), 1)
    outs = []
    for h in range(2):
        hs = slice(h * MLA_HEAD_PAD, (h + 1) * MLA_HEAD_PAD)
        q = q_ref[:, hs]

        def step(j, carry, mask=None, hs=hs, q=q):
            rows = pl.ds(pl.multiple_of(j * tq, tq), tq)
            return _flash_step(q, k_ref[rows, hs], v_ref[rows, :], carry, mask)

        carry = step(i, _flash_init(tq), _causal_mask(tq))
        _, l, acc = lax.fori_loop(0, i, step, carry)
        outs.append(acc / l)
    o_ref[...] = jnp.where(lane < MLA_V, outs[0], outs[1]).astype(o_ref.dtype)


def _mla_attention(qm, km, vm, batch, seq):
    nq = seq // TQ
    return pl.pallas_call(
        _mla_body,
        grid=(batch, MLA_HEADS // 2, nq),
        in_specs=[pl.BlockSpec((TQ, 2 * MLA_HEAD_PAD), lambda b, g, i: (b * nq + i, g)),
                  pl.BlockSpec((seq, 2 * MLA_HEAD_PAD), lambda b, g, i: (b, g)),
                  pl.BlockSpec((seq, LANES), lambda b, g, i: (b, g))],
        out_specs=pl.BlockSpec((TQ, LANES), lambda b, g, i: (b * nq + i, g)),
        out_shape=jax.ShapeDtypeStruct((batch * seq, MLA_W), BF16),
        compiler_params=_cparams("parallel", "parallel", "arbitrary"),
        name="mla_attn",
    )(qm, km, vm)


def _moba_body(n_kb, q_ref, k_ref, v_ref, sel_ref, o_ref):
    g = pl.program_id(1)
    i = pl.program_id(2)
    tq = q_ref.shape[0]
    lane = lax.broadcasted_iota(jnp.int32, (tq, LANES), 1)
    q_all = q_ref[...]
    outs = []
    for h in range(2):
        q = jnp.where((lane // HEAD_DIM) == h, q_all, jnp.zeros_like(q_all))
        col0 = (2 * g + h) * n_kb

        def step(j, carry, q=q, col0=col0):
            rows = pl.ds(pl.multiple_of(j * tq, tq), tq)
            picked = jnp.sum(jnp.where(lane == col0 + j, sel_ref[...], 0.0), axis=1, keepdims=True)
            return _flash_step(q, k_ref[rows, :], v_ref[rows, :], carry, picked > 0.0)

        own = pl.ds(pl.multiple_of(i * tq, tq), tq)
        carry = _flash_step(q, k_ref[own, :], v_ref[own, :], _flash_init(tq), _causal_mask(tq))
        _, l, acc = lax.fori_loop(0, i, step, carry)
        outs.append(acc / l)
    o_ref[...] = jnp.where(lane < HEAD_DIM, outs[0], outs[1]).astype(o_ref.dtype)


def _moba_attention(qb, kb, vb, sel, batch, seq):
    nq = seq // TQ
    blk = pl.BlockSpec((TQ, LANES), lambda b, g, i: (b * nq + i, g))
    res = pl.BlockSpec((seq, LANES), lambda b, g, i: (b, g))
    return pl.pallas_call(
        functools.partial(_moba_body, seq // MOBA_BLOCK),
        grid=(batch, MOBA_HEADS // 2, nq),
        in_specs=[blk, res, res, pl.BlockSpec((TQ, LANES), lambda b, g, i: (b * nq + i, 0))],
        out_specs=blk,
        out_shape=jax.ShapeDtypeStruct((batch * seq, MOBA_W), BF16),
        compiler_params=_cparams("parallel", "parallel", "arbitrary"),
        name="moba_attn",
    )(qb, kb, vb, sel)


def _dil_body(seq, q_ref, k_ref, v_ref, o_ref, ob_ref, lse_ref):
    w = DIL_W_UNITS
    lane = lax.broadcasted_iota(jnp.int32, (w, LANES), 1)
    row = lax.broadcasted_iota(jnp.int32, (w, w), 0)
    col = lax.broadcasted_iota(jnp.int32, (w, w), 1)
    prev_ok = col >= row
    cur_ok = col <= row
    for bi, (_, dil) in enumerate(DIL_PAIRS):
        n_blk = seq // (dil * w)

        def rows_of(r, n, dil=dil):
            start = r + n * (w * dil)
            return pl.ds(start, w, stride=dil) if dil > 1 else pl.ds(pl.multiple_of(start, w), w)

        def one(idx, _, bi=bi, n_blk=n_blk, rows_of=rows_of):
            r = idx // n_blk
            n = idx % n_blk
            cur = rows_of(r, n)
            prev = rows_of(r, jnp.maximum(n - 1, 0))
            q_all = q_ref[cur, :].astype(BF16)
            kc, vc = k_ref[cur, :].astype(BF16), v_ref[cur, :].astype(BF16)
            kp, vp = k_ref[prev, :].astype(BF16), v_ref[prev, :].astype(BF16)
            has_prev = n > 0
            outs, lses = [], []
            for h in range(2):
                q = jnp.where((lane // HEAD_DIM) == h, q_all, jnp.zeros_like(q_all))
                dn = (((1,), (1,)), ((), ()))
                s_p = jnp.where(prev_ok & has_prev, lax.dot_general(q, kp, dn, preferred_element_type=F32), NEG_INF)
                s_c = jnp.where(cur_ok, lax.dot_general(q, kc, dn, preferred_element_type=F32), NEG_INF)
                m = jnp.maximum(jnp.max(s_p, axis=1, keepdims=True), jnp.max(s_c, axis=1, keepdims=True))
                e_p, e_c = jnp.exp(s_p - m), jnp.exp(s_c - m)
                l = jnp.sum(e_p, axis=1, keepdims=True) + jnp.sum(e_c, axis=1, keepdims=True)
                o = (jnp.dot(e_p.astype(BF16), vp, preferred_element_type=F32)
                     + jnp.dot(e_c.astype(BF16), vc, preferred_element_type=F32)) / l
                outs.append(o)
                lses.append(jnp.broadcast_to(m + jnp.log(l), (w, LANES)))
            ob_ref[bi, cur, :] = jnp.where(lane < HEAD_DIM, outs[0], outs[1])
            lse_ref[bi, cur, :] = jnp.where(lane < HEAD_DIM, lses[0], lses[1])
            return 0

        lax.fori_loop(0, seq // w, one, 0)

    def merge(c, _):
        rows = pl.ds(pl.multiple_of(c * TQ, TQ), TQ)
        ls = [lse_ref[b, rows, :] for b in range(len(DIL_PAIRS))]
        top = functools.reduce(jnp.maximum, ls)
        ws = [jnp.exp(x - top) for x in ls]
        num = sum(wb * ob_ref[b, rows, :] for b, wb in enumerate(ws))
        o_ref[rows, :] = (num / sum(ws)).astype(o_ref.dtype)
        return 0

    lax.fori_loop(0, seq // TQ, merge, 0)


def _dil_attention(qc, kc, vc, batch, seq):
    blk = pl.BlockSpec((seq, LANES), lambda b, g: (b, g))
    nbr = len(DIL_PAIRS)
    return pl.pallas_call(
        functools.partial(_dil_body, seq),
        grid=(batch, DIL_HEADS // 2),
        in_specs=[blk, blk, blk],
        out_specs=blk,
        out_shape=jax.ShapeDtypeStruct((batch * seq, DIL_W), BF16),
        scratch_shapes=[pltpu.VMEM((nbr, seq, LANES), F32), pltpu.VMEM((nbr, seq, LANES), F32)],
        compiler_params=_cparams("parallel", "parallel"),
        name="dil_attn",
    )(qc, kc, vc)


def _layer_norm(y, g, b):
    mu = jnp.mean(y, axis=-1, keepdims=True)
    yc = y - mu
    var = jnp.mean(jnp.square(yc), axis=-1, keepdims=True)
    return yc * lax.rsqrt(var + LN_EPS) * g + b


def _first_argmax(rows):
    best, idx = rows[0], jnp.zeros(rows[0].shape, jnp.int32)
    for j in range(1, len(rows)):
        upd = rows[j] > best
        idx = jnp.where(upd, j, idx)
        best = jnp.where(upd, rows[j], best)
    return best, idx


def _pick(rows, idx):
    out = rows[0]
    for j in range(1, len(rows)):
        out = jnp.where(idx == j, rows[j], out)
    return out


def _top2(rows):
    v1, i1 = _first_argmax(rows)
    rest = [jnp.where(i1 == j, NEG_INF, r) for j, r in enumerate(rows)]
    v2, i2 = _first_argmax(rest)
    return v1, i1, v2, i2


def _outproj_body(oa_ref, ob_ref, oc_ref, wo_ref, x_ref, g_ref, b_ref, rwt_ref, rb_ref,
                  x1_ref, route_ref, cnt_ref):
    i = pl.program_id(0)
    tm = x_ref.shape[0]

    @pl.when(i == 0)
    def _():
        cnt_ref[...] = jnp.zeros_like(cnt_ref)

    mix = (jnp.dot(oa_ref[...], wo_ref[0:MLA_W, :], preferred_element_type=F32)
           + jnp.dot(ob_ref[...], wo_ref[MLA_W:MLA_W + MOBA_W, :], preferred_element_type=F32)
           + jnp.dot(oc_ref[...], wo_ref[MLA_W + MOBA_W:, :], preferred_element_type=F32))
    x1 = _layer_norm(ALPHA * x_ref[...] + mix, g_ref[...], b_ref[...])
    x1_ref[...] = x1

    logits = lax.dot_general(rwt_ref[...], x1, (((1,), (1,)), ((), ())),
                             precision=lax.Precision.HIGHEST, preferred_element_type=F32)
    s = jax.nn.sigmoid(logits)
    sb = s + rb_ref[...]
    s_rows = [s[e:e + 1, :] for e in range(N_EXPERTS)]
    sb_rows = [sb[e:e + 1, :] for e in range(N_EXPERTS)]
    grp = lambda rows, g: rows[g * EXP_PER_GROUP:(g + 1) * EXP_PER_GROUP]
    g_scores = []
    for g in range(N_GROUPS):
        v1, _, v2, _ = _top2(grp(sb_rows, g))
        g_scores.append(v1 + v2)
    _, g_sel = _first_argmax(g_scores)
    in_b = [_pick([grp(sb_rows, g)[j] for g in range(N_GROUPS)], g_sel) for j in range(EXP_PER_GROUP)]
    in_s = [_pick([grp(s_rows, g)[j] for g in range(N_GROUPS)], g_sel) for j in range(EXP_PER_GROUP)]
    _, j1, _, j2 = _top2(in_b)
    s1, s2 = _pick(in_s, j1), _pick(in_s, j2)
    den = s1 + s2
    e1 = g_sel * EXP_PER_GROUP + j1
    e2 = g_sel * EXP_PER_GROUP + j2
    route_ref[...] = jnp.concatenate(
        [e1.astype(F32), e2.astype(F32), s1 / den, s2 / den, jnp.zeros((4, tm), F32)], axis=0)
    eio = lax.broadcasted_iota(jnp.int32, (N_EXPERTS, tm), 0)
    hits = jnp.where(eio == e1, 1.0, 0.0) + jnp.where(eio == e2, 1.0, 0.0)
    cnt_ref[...] += jnp.sum(hits, axis=1, keepdims=True)


def _outproj_ln_router(o_a, o_b, o_c, w_out_b, x2, ln_g, ln_b, rwt, rb):
    t = x2.shape[0]
    tm = TM_PROJ
    row = lambda w: pl.BlockSpec((tm, w), lambda i: (i, 0))
    full = lambda a: pl.BlockSpec(a.shape, lambda i: (0,) * a.ndim)
    return pl.pallas_call(
        _outproj_body,
        grid=(t // tm,),
        in_specs=[row(MLA_W), row(MOBA_W), row(DIL_W), full(w_out_b), row(D_MODEL), full(ln_g), full(ln_b),
                  full(rwt), full(rb)],
        out_specs=[row(D_MODEL), pl.BlockSpec((8, tm), lambda i: (0, i)),
                   pl.BlockSpec((N_EXPERTS, LANES), lambda i: (0, 0))],
        out_shape=[jax.ShapeDtypeStruct((t, D_MODEL), F32), jax.ShapeDtypeStruct((8, t), F32),
                   jax.ShapeDtypeStruct((N_EXPERTS, LANES), F32)],
        compiler_params=_cparams("arbitrary"),
        name="outproj_ln_router",
    )(o_a, o_b, o_c, w_out_b, x2, ln_g, ln_b, rwt, rb)


TN_SORT = 512
TM_ROWS = 256


def _dest_body(route_ref, pstart_ref, dest_ref, carry_ref):
    n = route_ref.shape[1]

    @pl.when(pl.program_id(0) == 0)
    def _():
        carry_ref[...] = jnp.broadcast_to(pstart_ref[...], carry_ref.shape)

    before = (lax.broadcasted_iota(jnp.int32, (n, n), 0) < lax.broadcasted_iota(jnp.int32, (n, n), 1))
    before = jnp.where(before, 1.0, 0.0).astype(BF16)
    eio = lax.broadcasted_iota(jnp.int32, (N_EXPERTS, n), 0)
    rows = []
    for k in range(TOP_K):
        e_k = route_ref[k:k + 1, :].astype(jnp.int32)
        hit = jnp.where(eio == e_k, 1.0, 0.0)
        earlier = jnp.dot(hit.astype(BF16), before, preferred_element_type=F32)
        carry = carry_ref[:, 0:1]
        rows.append(jnp.sum(hit * (earlier + carry), axis=0, keepdims=True))
        carry_ref[...] += jnp.sum(hit, axis=1, keepdims=True)
    dest_ref[...] = jnp.concatenate(rows, axis=0).astype(jnp.int32)


def _dest_rows(route, pstart):
    t = route.shape[1]
    return pl.pallas_call(
        _dest_body,
        grid=(t // TN_SORT,),
        in_specs=[pl.BlockSpec((8, TN_SORT), lambda i: (0, i)),
                  pl.BlockSpec((N_EXPERTS, 1), lambda i: (0, 0))],
        out_specs=pl.BlockSpec((TOP_K, TN_SORT), lambda i: (0, i)),
        out_shape=jax.ShapeDtypeStruct((TOP_K, t), jnp.int32),
        scratch_shapes=[pltpu.VMEM((N_EXPERTS, LANES), F32)],
        compiler_params=_cparams("arbitrary"),
        name="moe_dest",
    )(route, pstart)


def _scatter_body(dest_ref, x_ref, xs_in_ref, xs_ref, sem):
    del xs_in_ref
    tm = x_ref.shape[0]

    def row_copy(r, d):
        return pltpu.make_async_copy(x_ref.at[pl.ds(r, 1)], xs_ref.at[pl.ds(d, 1)], sem)

    def issue(r, _):
        for k in range(TOP_K):
            row_copy(r, dest_ref[k, r]).start()
        return 0

    def drain(r, _):
        for k in range(TOP_K):
            row_copy(r, dest_ref[k, r]).wait()
        return 0

    lax.fori_loop(0, tm, issue, 0)
    lax.fori_loop(0, tm, drain, 0)


def _scatter_rows(dest, x1, n_rows):
    t, d = x1.shape
    xs_init = jnp.zeros((n_rows, d), x1.dtype)
    return pl.pallas_call(
        _scatter_body,
        grid=(t // TM_ROWS,),
        in_specs=[pl.BlockSpec((TOP_K, TM_ROWS), lambda i: (0, i), memory_space=pltpu.SMEM),
                  pl.BlockSpec((TM_ROWS, d), lambda i: (i, 0)),
                  pl.BlockSpec(memory_space=pl.ANY)],
        out_specs=pl.BlockSpec(memory_space=pl.ANY),
        out_shape=jax.ShapeDtypeStruct((n_rows, d), x1.dtype),
        scratch_shapes=[pltpu.SemaphoreType.DMA(())],
        input_output_aliases={2: 0},
        compiler_params=_cparams("arbitrary"),
        name="moe_scatter",
    )(dest, x1, xs_init)


def _expert_body(blk_exp_ref, nused_ref, xs_ref, w1_ref, w3_ref, w2_ref, y_ref):
    del blk_exp_ref
    i = pl.program_id(0)

    @pl.when(i < nused_ref[0])
    def _():
        xb = xs_ref[...].astype(BF16)
        h1 = jnp.dot(xb, w1_ref[...].astype(BF16), preferred_element_type=F32)
        h3 = jnp.dot(xb, w3_ref[...].astype(BF16), preferred_element_type=F32)
        hb = (jax.nn.silu(h1) * h3).astype(BF16)
        y_ref[...] = jnp.dot(hb, w2_ref[...].astype(BF16), preferred_element_type=F32)

    @pl.when(i >= nused_ref[0])
    def _():
        y_ref[...] = jnp.zeros_like(y_ref)


def _expert_ffn(blk_exp, nused, xs, w1, w3, w2):
    n_rows, d = xs.shape
    n_blk = n_rows // MOE_BLOCK
    grid_spec = pltpu.PrefetchScalarGridSpec(
        num_scalar_prefetch=2,
        grid=(n_blk,),
        in_specs=[pl.BlockSpec((MOE_BLOCK, d), lambda i, be, nu: (i, 0)),
                  pl.BlockSpec((None, d, D_EXPERT), lambda i, be, nu: (be[i], 0, 0)),
                  pl.BlockSpec((None, d, D_EXPERT), lambda i, be, nu: (be[i], 0, 0)),
                  pl.BlockSpec((None, D_EXPERT, d), lambda i, be, nu: (be[i], 0, 0))],
        out_specs=pl.BlockSpec((MOE_BLOCK, d), lambda i, be, nu: (i, 0)),
    )
    return pl.pallas_call(
        _expert_body,
        grid_spec=grid_spec,
        out_shape=jax.ShapeDtypeStruct((n_rows, d), F32),
        compiler_params=_cparams("arbitrary"),
        name="moe_experts",
    )(blk_exp, nused, xs, w1, w3, w2)


def _combine_body(dest_ref, route_ref, x1_ref, g_ref, b_ref, ys_ref, o_ref, buf_ref, sem):
    tm = x1_ref.shape[0]

    def row_copy(k, r):
        return pltpu.make_async_copy(ys_ref.at[pl.ds(dest_ref[k, r], 1)], buf_ref.at[k, pl.ds(r, 1)], sem)

    def issue(r, _):
        for k in range(TOP_K):
            row_copy(k, r).start()
        return 0

    def drain(r, _):
        for k in range(TOP_K):
            row_copy(k, r).wait()
        return 0

    lax.fori_loop(0, tm, issue, 0)
    lax.fori_loop(0, tm, drain, 0)
    ffn = jnp.zeros((tm, D_MODEL), F32)
    for k in range(TOP_K):
        gate = jnp.broadcast_to(route_ref[TOP_K + k:TOP_K + k + 1, :], (LANES, tm)).T
        ffn = ffn + jnp.tile(gate, (1, D_MODEL // LANES)) * buf_ref[k]
    o_ref[...] = _layer_norm(ALPHA * x1_ref[...] + ffn, g_ref[...], b_ref[...])


def _combine_ln(dest, route, x1, ln_g, ln_b, ys):
    t, d = x1.shape
    full = lambda a: pl.BlockSpec(a.shape, lambda i: (0,) * a.ndim)
    return pl.pallas_call(
        _combine_body,
        grid=(t // TM_ROWS,),
        in_specs=[pl.BlockSpec((TOP_K, TM_ROWS), lambda i: (0, i), memory_space=pltpu.SMEM),
                  pl.BlockSpec((8, TM_ROWS), lambda i: (0, i)),
                  pl.BlockSpec((TM_ROWS, d), lambda i: (i, 0)), full(ln_g), full(ln_b),
                  pl.BlockSpec(memory_space=pl.ANY)],
        out_specs=pl.BlockSpec((TM_ROWS, d), lambda i: (i, 0)),
        out_shape=jax.ShapeDtypeStruct((t, d), F32),
        scratch_shapes=[pltpu.VMEM((TOP_K, TM_ROWS, d), F32), pltpu.SemaphoreType.DMA(())],
        compiler_params=_cparams("arbitrary"),
        name="moe_combine_ln",
    )(dest, route, x1, ln_g, ln_b, ys)


def _moe_layer(x1, route, counts, w1, w3, w2, ln_g, ln_b):
    t = x1.shape[0]
    n_rows = t * TOP_K + N_EXPERTS * MOE_BLOCK
    n_blk = n_rows // MOE_BLOCK
    cnt = counts[:, 0].astype(jnp.int32)
    padded = (cnt + MOE_BLOCK - 1) // MOE_BLOCK * MOE_BLOCK
    pend = jnp.cumsum(padded)
    pstart = (pend - padded).astype(F32).reshape(N_EXPERTS, 1)
    blk_exp = jnp.minimum(jnp.searchsorted(pend, jnp.arange(n_blk, dtype=jnp.int32) * MOE_BLOCK, side="right"),
                          N_EXPERTS - 1).astype(jnp.int32)
    nused = (pend[-1:] // MOE_BLOCK).astype(jnp.int32)
    dest = _dest_rows(route, pstart)
    xs = _scatter_rows(dest, x1, n_rows)
    ys = _expert_ffn(blk_exp, nused, xs, w1, w3, w2)
    return _combine_ln(dest, route, x1, ln_g, ln_b, ys)


def _rope_tables(seq):
    def tab(dim):
        inv = ROPE_THETA ** (-jnp.arange(0, dim, 2, dtype=F32) / dim)
        ang = jnp.arange(seq, dtype=F32)[:, None] * inv[None, :]
        return jnp.cos(ang), jnp.sin(ang)

    c, s = tab(HEAD_DIM)
    cos64 = jnp.tile(jnp.concatenate([c, c], axis=1), (1, LANES // HEAD_DIM))
    sin64 = jnp.tile(jnp.concatenate([-s, s], axis=1), (1, LANES // HEAD_DIM))
    c, s = tab(MLA_ROPE)
    pad_l, pad_r = MLA_ROPE_LANE, LANES - MLA_ROPE_LANE - MLA_ROPE
    cosm = jnp.concatenate([jnp.ones((seq, pad_l), F32), c, c, jnp.ones((seq, pad_r), F32)], axis=1)
    sinm = jnp.concatenate([jnp.zeros((seq, pad_l), F32), -s, s, jnp.zeros((seq, pad_r), F32)], axis=1)
    return cos64, sin64, cosm, sinm


def _prep_mixer_weights(w_in, q_norm, w_uq, kv_norm, w_ukv):
    d = w_in.shape[0]
    lat = MLA_Q_LORA + MLA_KV_LORA
    w_in_p = jnp.concatenate([
        w_in[:, :lat], jnp.zeros((d, MLA_ROPE_LANE), F32), w_in[:, lat:lat + MLA_ROPE],
        jnp.zeros((d, LANES - MLA_ROPE_LANE - MLA_ROPE), F32), w_in[:, lat + MLA_ROPE:]], axis=1).astype(BF16)
    wq = w_uq.reshape(MLA_Q_LORA, MLA_HEADS, MLA_NOPE + MLA_ROPE)
    wuq_p = jnp.pad(wq, ((0, 0), (0, 0), (0, MLA_HEAD_PAD - MLA_NOPE - MLA_ROPE)))
    wuq_p = wuq_p.reshape(MLA_Q_LORA, MLA_HEADS * MLA_HEAD_PAD).astype(BF16)
    wkv = w_ukv.reshape(MLA_KV_LORA, MLA_HEADS, MLA_NOPE + MLA_V)
    wk = jnp.pad(wkv[:, :, :MLA_NOPE], ((0, 0), (0, 0), (0, MLA_HEAD_PAD - MLA_NOPE)))
    wukv_p = jnp.concatenate([wk.reshape(MLA_KV_LORA, MLA_HEADS * MLA_HEAD_PAD),
                              wkv[:, :, MLA_NOPE:].reshape(MLA_KV_LORA, MLA_W)], axis=1).astype(BF16)
    return w_in_p, q_norm.reshape(1, -1), wuq_p, kv_norm.reshape(1, -1), wukv_p


def _mixer_heads(x2, w_in, q_norm, w_uq, kv_norm, w_ukv, tabs, batch, seq):
    w_in_p, qn, wuq_p, kvn, wukv_p = _prep_mixer_weights(w_in, q_norm, w_uq, kv_norm, w_ukv)
    qm, km, vm, qb, kb, vb, qc, kc, vc, sel = _inproj(x2, w_in_p, qn, kvn, wuq_p, wukv_p, tabs, seq)
    o_a = _mla_attention(qm, km, vm, batch, seq)
    o_b = _moba_attention(qb, kb, vb, sel, batch, seq)
    o_c = _dil_attention(qc, kc, vc, batch, seq)
    return o_a, o_b, o_c


def kernel(x, w_in, mla_q_norm, mla_w_uq, mla_kv_norm, mla_w_ukv, w_out, ln1_g, ln1_b, router_w, router_b, moe_w1, moe_w3, moe_w2, ln2_g, ln2_b):
    batch, seq, d = x.shape
    tabs = _rope_tables(seq)
    x2 = x.reshape(batch * seq, d)
    rwt = router_w.T
    rb = router_b.reshape(N_EXPERTS, 1)
    for l in range(DEPTH):
        o_a, o_b, o_c = _mixer_heads(x2, w_in[l], mla_q_norm[l], mla_w_uq[l], mla_kv_norm[l], mla_w_ukv[l],
                                     tabs, batch, seq)
        x1, route, counts = _outproj_ln_router(o_a, o_b, o_c, w_out[l].astype(BF16), x2,
                                               ln1_g[l].reshape(1, d), ln1_b[l].reshape(1, d), rwt, rb)
        x2 = _moe_layer(x1, route, counts, moe_w1[l], moe_w3[l], moe_w2[l],
                        ln2_g[l].reshape(1, d), ln2_b[l].reshape(1, d))
    return x2.reshape(batch, seq, d)
```

```python
import functools

import jax
import jax.numpy as jnp
import numpy as np
from jax import lax
from jax.experimental import pallas as pl
from jax.experimental.pallas import tpu as pltpu

F32 = jnp.float32
BF16 = jnp.bfloat16
LANES = 128
NEG_INF = float("-inf")
VMEM_LIMIT_BYTES = 56 * 1024 * 1024

D_MODEL = 1024
DEPTH = 2
HEAD_DIM = 64
ROPE_THETA = 10000.0
MLA_HEADS = 4
MLA_NOPE = 64
MLA_ROPE = 32
MLA_V = 64
MLA_Q_LORA = 256
MLA_KV_LORA = 128
MOBA_HEADS = 6
MOBA_BLOCK = 256
MOBA_TOPK = 3
DIL_HEADS = 6
DIL_PAIRS = ((128, 1), (512, 4), (2048, 16))
MLA_W = MLA_HEADS * MLA_V
MOBA_W = MOBA_HEADS * HEAD_DIM
DIL_W = DIL_HEADS * HEAD_DIM
N_EXPERTS = 64
N_GROUPS = 8
EXP_PER_GROUP = N_EXPERTS // N_GROUPS
TOP_K = 2
D_EXPERT = 256
MOE_BLOCK = 128
ALPHA = (2 * DEPTH) ** 0.25
LN_EPS = 1e-5
RMS_EPS = 1e-6

SEG_A = 512
SEG_B = 3 * MOBA_W
SEG_C = 3 * DIL_W
N_IN_PAD = SEG_A + SEG_B + SEG_C
MLA_HEAD_PAD = 128
MLA_ROPE_LANE = 64

TM_PROJ = 512
TQ = 256
KB_WIDE = 4
DIL_W_UNITS = 128
LOG2_E = 1.4426950408889634
ONES_ROW = HEAD_DIM
SEL_LANE = HEAD_DIM
MASK_BIG = 2.0 ** 100


def _cparams(*sem):
    return pltpu.CompilerParams(dimension_semantics=sem, vmem_limit_bytes=VMEM_LIMIT_BYTES)


def _rope(x, cos, sin_signed, half):
    lane = lax.broadcasted_iota(jnp.int32, x.shape, 1)
    first = ((lane // half) % 2) == 0
    rot = jnp.where(first, pltpu.roll(x, LANES - half, 1), pltpu.roll(x, half, 1))
    return x * cos + rot * sin_signed


def _rms(x, g):
    return x * lax.rsqrt(jnp.mean(jnp.square(x), axis=-1, keepdims=True) + RMS_EPS) * g


def _inproj_body(seq_blocks, x_ref, w_ref, qn_ref, kvn_ref, wuq_ref, wukv_ref,
                 cos64_ref, sin64_ref, cosm_ref, sinm_ref,
                 qm_ref, km_ref, vm_ref, qb_ref, kb_ref, vb_ref, qc_ref, kc_ref, vc_ref,
                 kmean_ref):
    i = pl.program_id(0)
    tm = x_ref.shape[0]
    n_kb = kmean_ref.shape[0]
    sblk = i % seq_blocks

    @pl.when(sblk == 0)
    def _():
        kmean_ref[...] = jnp.zeros_like(kmean_ref)

    xb = x_ref[...].astype(BF16)
    cos64, sin64 = cos64_ref[...], sin64_ref[...]
    cosm, sinm = cosm_ref[...], sinm_ref[...]

    h_a = jnp.dot(xb, w_ref[:, 0:SEG_A], preferred_element_type=F32)
    cq = _rms(h_a[:, 0:MLA_Q_LORA], qn_ref[...]).astype(BF16)
    ckv = _rms(h_a[:, MLA_Q_LORA:MLA_Q_LORA + MLA_KV_LORA], kvn_ref[...]).astype(BF16)
    kr = _rope(h_a[:, MLA_Q_LORA + MLA_KV_LORA:SEG_A], cosm, sinm, MLA_ROPE // 2)
    q_m = jnp.dot(cq, wuq_ref[...], preferred_element_type=F32)
    kv_m = jnp.dot(ckv, wukv_ref[...], preferred_element_type=F32)
    mla_scale = (MLA_NOPE + MLA_ROPE) ** -0.5 * LOG2_E
    lane_t = lax.broadcasted_iota(jnp.int32, (tm, LANES), 1)
    row_t = lax.broadcasted_iota(jnp.int32, (LANES, tm), 0)
    zeros_half = jnp.zeros((LANES - HEAD_DIM, tm), F32)

    def put_t(ref, h, val_t):
        for b in range(tm // TQ):
            ref[b, h * LANES:(h + 1) * LANES, :] = val_t[:, b * TQ:(b + 1) * TQ].astype(ref.dtype)

    def v_heads_t(pair):
        pt = pair.T
        return [jnp.where(row_t == ONES_ROW, 1.0, jnp.concatenate([pt[s * HEAD_DIM:(s + 1) * HEAD_DIM], zeros_half], 0))
                for s in range(2)]

    for h in range(MLA_HEADS):
        sl = slice(h * MLA_HEAD_PAD, (h + 1) * MLA_HEAD_PAD)
        put_t(qm_ref, h, (_rope(q_m[:, sl], cosm, sinm, MLA_ROPE // 2) * mla_scale).T)
        km_ref[:, sl] = (kv_m[:, sl] + kr).astype(BF16)
    for c in range(MLA_HEADS // 2):
        v_pair = kv_m[:, (MLA_HEADS + c) * LANES:(MLA_HEADS + c + 1) * LANES]
        for s, v_t in enumerate(v_heads_t(v_pair)):
            put_t(vm_ref, 2 * c + s, v_t)

    h_b = jnp.dot(xb, w_ref[:, SEG_A:SEG_A + SEG_B], preferred_element_type=F32)
    scale = HEAD_DIM ** -0.5
    lane = lax.broadcasted_iota(jnp.int32, (n_kb, LANES), 1)
    pos = sblk * tm + lax.broadcasted_iota(jnp.int32, (1, tm), 1)
    q_blk = pos // MOBA_BLOCK
    jio = lax.broadcasted_iota(jnp.int32, (n_kb, tm), 0)
    elig = jio < q_blk
    row_blk = (sblk * tm + lax.broadcasted_iota(jnp.int32, (tm, LANES), 0)) // MOBA_BLOCK
    blk_lane = lane_t == SEL_LANE + row_blk
    first_head = lane_t < HEAD_DIM

    def split(pair):
        return (jnp.where(first_head, pair, 0.0), jnp.where(first_head, pltpu.roll(pair, HEAD_DIM, 1), 0.0))

    for c in range(MOBA_W // LANES):
        sl = slice(c * LANES, (c + 1) * LANES)
        q = _rope(h_b[:, sl], cos64, sin64, HEAD_DIM // 2)
        k = _rope(h_b[:, MOBA_W + c * LANES:MOBA_W + (c + 1) * LANES], cos64, sin64, HEAD_DIM // 2)
        km = kmean_ref[:, sl]
        kb_row = lax.broadcasted_iota(jnp.int32, (n_kb, LANES), 0)
        for b in range(tm // MOBA_BLOCK):
            mean_b = jnp.mean(k[b * MOBA_BLOCK:(b + 1) * MOBA_BLOCK], axis=0, keepdims=True)
            km = jnp.where(kb_row == sblk * (tm // MOBA_BLOCK) + b, mean_b, km)
        kmean_ref[:, sl] = km
        q_t = (q * (scale * LOG2_E)).T
        k_heads = split(k)
        v_t = v_heads_t(h_b[:, 2 * MOBA_W + c * LANES:2 * MOBA_W + (c + 1) * LANES])
        for half in range(2):
            hs = slice((2 * c + half) * LANES, (2 * c + half + 1) * LANES)
            in_head = (lane // HEAD_DIM) == half
            gate = lax.dot_general(jnp.where(in_head, km, 0.0), q, (((1,), (1,)), ((), ())),
                                   precision=lax.Precision.HIGHEST,
                                   preferred_element_type=F32)
            gate = jnp.where(elig, gate, NEG_INF)
            q_rows = [q_t[half * HEAD_DIM:(half + 1) * HEAD_DIM]]
            for j in range(n_kb):
                gj = gate[j:j + 1, :]
                beats = (gate > gj) | ((gate == gj) & (jio < j))
                cnt = jnp.sum(beats.astype(F32), axis=0, keepdims=True)
                keep = ((cnt < MOBA_TOPK) & elig[j:j + 1, :]) | (q_blk == j)
                q_rows.append(jnp.where(keep, 0.0, -MASK_BIG))
            q_rows.append(jnp.zeros((LANES - SEL_LANE - n_kb, tm), F32))
            put_t(qb_ref, 2 * c + half, jnp.concatenate(q_rows, axis=0))
            kb_ref[:, hs] = jnp.where(blk_lane, 1.0, k_heads[half]).astype(BF16)
            put_t(vb_ref, 2 * c + half, v_t[half])

    h_c = jnp.dot(xb, w_ref[:, SEG_A + SEG_B:], preferred_element_type=F32)
    for c in range(DIL_W // LANES):
        sl = slice(c * LANES, (c + 1) * LANES)
        qc_ref[:, sl] = _rope(h_c[:, sl], cos64, sin64, HEAD_DIM // 2) * scale
        kc_ref[:, sl] = _rope(h_c[:, DIL_W + c * LANES:DIL_W + (c + 1) * LANES], cos64, sin64, HEAD_DIM // 2)
    vc_ref[...] = h_c[:, 2 * DIL_W:]


def _inproj(x2, w_in_p, qn, kvn, wuq_p, wukv_p, tabs, seq):
    t = x2.shape[0]
    tm = TM_PROJ
    seq_blocks = seq // tm
    n_kb = seq // MOBA_BLOCK
    row = lambda w: pl.BlockSpec((tm, w), lambda i: (i, 0))
    full = lambda a: pl.BlockSpec(a.shape, lambda i: (0,) * a.ndim)
    tab = pl.BlockSpec((tm, LANES), lambda i: (i % seq_blocks, 0))
    out_shapes = [
        jax.ShapeDtypeStruct((t // TQ, MLA_HEADS * LANES, TQ), BF16),
        jax.ShapeDtypeStruct((t, MLA_HEADS * MLA_HEAD_PAD), BF16),
        jax.ShapeDtypeStruct((t // TQ, MLA_HEADS * LANES, TQ), BF16),
        jax.ShapeDtypeStruct((t // TQ, MOBA_HEADS * LANES, TQ), BF16),
        jax.ShapeDtypeStruct((t, MOBA_HEADS * LANES), BF16),
        jax.ShapeDtypeStruct((t // TQ, MOBA_HEADS * LANES, TQ), BF16),
        jax.ShapeDtypeStruct((t, DIL_W), F32),
        jax.ShapeDtypeStruct((t, DIL_W), F32),
        jax.ShapeDtypeStruct((t, DIL_W), F32),
    ]
    return pl.pallas_call(
        functools.partial(_inproj_body, seq_blocks),
        grid=(t // tm,),
        in_specs=[row(D_MODEL), full(w_in_p), full(qn), full(kvn), full(wuq_p), full(wukv_p),
                  tab, tab, tab, tab],
        out_specs=[row(s.shape[1]) if len(s.shape) == 2 else
                   pl.BlockSpec((tm // TQ,) + s.shape[1:], lambda i: (i, 0, 0)) for s in out_shapes],
        out_shape=out_shapes,
        scratch_shapes=[pltpu.VMEM((n_kb, MOBA_W), F32)],
        compiler_params=_cparams("arbitrary"),
        name="inproj",
    )(x2, w_in_p, qn, kvn, wuq_p, wukv_p, *tabs)


def _flash_body(n_heads, qt_ref, k_ref, vt_ref, o_ref, *state):
    i = pl.program_id(1)
    tq = qt_ref.shape[1]

    def head(h):
        return slice(h * LANES, (h + 1) * LANES)

    m_refs, acc_refs = state[:n_heads], state[n_heads:]
    key_le_query = (lax.broadcasted_iota(jnp.int32, (tq, tq), 0) <= lax.broadcasted_iota(jnp.int32, (tq, tq), 1))

    def scores(j, n, h):
        rows = pl.ds(pl.multiple_of(j * tq, tq), n * tq)
        return jnp.dot(k_ref[rows, head(h)], qt_ref[head(h), :], preferred_element_type=F32)

    def pv(j, n, h, p):
        return sum(jnp.dot(vt_ref[j + b, head(h), :], p[b * tq:(b + 1) * tq], preferred_element_type=F32)
                   for b in range(n))

    def own_block(h, s):
        s = jnp.where(key_le_query, s, NEG_INF)
        m = jnp.max(s, axis=0, keepdims=True)
        acc_refs[h][...] = pv(i, 1, h, jnp.exp2(s - m).astype(BF16))
        m_refs[h][...] = m

    def past_blocks(j, n, h, s):
        m_old = m_refs[h][...]
        m_new = jnp.maximum(m_old, jnp.max(s, axis=0, keepdims=True))
        p = jnp.exp2(s - m_new).astype(BF16)
        acc_refs[h][...] = jnp.exp2(m_old - m_new) * acc_refs[h][...] + pv(j, n, h, p)
        m_refs[h][...] = m_new

    def sweep(j, n, update):
        s_next = scores(j, n, 0)
        for h in range(n_heads):
            s = s_next
            if h + 1 < n_heads:
                s_next = scores(j, n, h + 1)
            update(h, s)

    sweep(i, 1, own_block)

    def wide_body(jj, _):
        sweep(jj * KB_WIDE, KB_WIDE, functools.partial(past_blocks, jj * KB_WIDE, KB_WIDE))
        return 0

    def narrow_body(j, _):
        sweep(j, 1, functools.partial(past_blocks, j, 1))
        return 0

    n_wide = i // KB_WIDE
    lax.fori_loop(0, n_wide, wide_body, 0)
    lax.fori_loop(n_wide * KB_WIDE, i, narrow_body, 0)

    for c in range(n_heads // 2):
        halves = []
        for h in (2 * c, 2 * c + 1):
            acc = acc_refs[h][...]
            halves.append(acc[0:HEAD_DIM] / acc[ONES_ROW:ONES_ROW + 1])
        o_ref[:, head(c)] = jnp.concatenate(halves, axis=0).T.astype(o_ref.dtype)


def _flash_attention(qt, k, vt, n_heads, batch, seq, name):
    nq = seq // TQ
    width = n_heads * LANES
    return pl.pallas_call(
        functools.partial(_flash_body, n_heads),
        grid=(batch, nq),
        in_specs=[pl.BlockSpec((None, width, TQ), lambda b, i: (b * nq + i, 0, 0)),
                  pl.BlockSpec((seq, width), lambda b, i: (b, 0)),
                  pl.BlockSpec((nq, width, TQ), lambda b, i: (b, 0, 0))],
        out_specs=pl.BlockSpec((TQ, n_heads * HEAD_DIM), lambda b, i: (b * nq + i, 0)),
        out_shape=jax.ShapeDtypeStruct((batch * seq, n_heads * HEAD_DIM), BF16),
        scratch_shapes=[pltpu.VMEM((1, TQ), F32)] * n_heads + [pltpu.VMEM((LANES, TQ), F32)] * n_heads,
        compiler_params=_cparams("parallel", "arbitrary"),
        name=name,
    )(qt, k, vt)


def _dil_body(seq, q_ref, k_ref, v_ref, o_ref, ob_ref, lse_ref):
    w = DIL_W_UNITS
    lane = lax.broadcasted_iota(jnp.int32, (w, LANES), 1)
    row = lax.broadcasted_iota(jnp.int32, (w, w), 0)
    col = lax.broadcasted_iota(jnp.int32, (w, w), 1)
    prev_ok = col >= row
    cur_ok = col <= row
    for bi, (_, dil) in enumerate(DIL_PAIRS):
        n_blk = seq // (dil * w)

        def rows_of(r, n, dil=dil):
            start = r + n * (w * dil)
            return pl.ds(start, w, stride=dil) if dil > 1 else pl.ds(pl.multiple_of(start, w), w)

        def one(idx, _, bi=bi, n_blk=n_blk, rows_of=rows_of):
            r = idx // n_blk
            n = idx % n_blk
            cur = rows_of(r, n)
            prev = rows_of(r, jnp.maximum(n - 1, 0))
            q_all = q_ref[cur, :].astype(BF16)
            kc, vc = k_ref[cur, :].astype(BF16), v_ref[cur, :].astype(BF16)
            kp, vp = k_ref[prev, :].astype(BF16), v_ref[prev, :].astype(BF16)
            has_prev = n > 0
            outs, lses = [], []
            for h in range(2):
                q = jnp.where((lane // HEAD_DIM) == h, q_all, jnp.zeros_like(q_all))
                dn = (((1,), (1,)), ((), ()))
                s_p = jnp.where(prev_ok & has_prev, lax.dot_general(q, kp, dn, preferred_element_type=F32), NEG_INF)
                s_c = jnp.where(cur_ok, lax.dot_general(q, kc, dn, preferred_element_type=F32), NEG_INF)
                m = jnp.maximum(jnp.max(s_p, axis=1, keepdims=True), jnp.max(s_c, axis=1, keepdims=True))
                e_p, e_c = jnp.exp(s_p - m), jnp.exp(s_c - m)
                l = jnp.sum(e_p, axis=1, keepdims=True) + jnp.sum(e_c, axis=1, keepdims=True)
                o = (jnp.dot(e_p.astype(BF16), vp, preferred_element_type=F32)
                     + jnp.dot(e_c.astype(BF16), vc, preferred_element_type=F32)) / l
                outs.append(o)
                lses.append(jnp.broadcast_to(m + jnp.log(l), (w, LANES)))
            ob_ref[bi, cur, :] = jnp.where(lane < HEAD_DIM, outs[0], outs[1])
            lse_ref[bi, cur, :] = jnp.where(lane < HEAD_DIM, lses[0], lses[1])
            return 0

        lax.fori_loop(0, seq // w, one, 0, unroll=4)

    def merge(c, _):
        rows = pl.ds(pl.multiple_of(c * TQ, TQ), TQ)
        ls = [lse_ref[b, rows, :] for b in range(len(DIL_PAIRS))]
        top = functools.reduce(jnp.maximum, ls)
        ws = [jnp.exp(x - top) for x in ls]
        num = sum(wb * ob_ref[b, rows, :] for b, wb in enumerate(ws))
        o_ref[rows, :] = (num / sum(ws)).astype(o_ref.dtype)
        return 0

    lax.fori_loop(0, seq // TQ, merge, 0)


def _dil_attention(qc, kc, vc, batch, seq):
    blk = pl.BlockSpec((seq, LANES), lambda b, g: (b, g))
    nbr = len(DIL_PAIRS)
    return pl.pallas_call(
        functools.partial(_dil_body, seq),
        grid=(batch, DIL_HEADS // 2),
        in_specs=[blk, blk, blk],
        out_specs=blk,
        out_shape=jax.ShapeDtypeStruct((batch * seq, DIL_W), BF16),
        scratch_shapes=[pltpu.VMEM((nbr, seq, LANES), F32), pltpu.VMEM((nbr, seq, LANES), F32)],
        compiler_params=_cparams("parallel", "parallel"),
        name="dil_attn",
    )(qc, kc, vc)


def _layer_norm(y, g, b):
    mu = jnp.mean(y, axis=-1, keepdims=True)
    yc = y - mu
    var = jnp.mean(jnp.square(yc), axis=-1, keepdims=True)
    return yc * lax.rsqrt(var + LN_EPS) * g + b


def _first_argmax(rows):
    best, idx = rows[0], jnp.zeros(rows[0].shape, jnp.int32)
    for j in range(1, len(rows)):
        upd = rows[j] > best
        idx = jnp.where(upd, j, idx)
        best = jnp.where(upd, rows[j], best)
    return best, idx


def _pick(rows, idx):
    out = rows[0]
    for j in range(1, len(rows)):
        out = jnp.where(idx == j, rows[j], out)
    return out


def _top2(rows):
    v1, i1 = _first_argmax(rows)
    rest = [jnp.where(i1 == j, NEG_INF, r) for j, r in enumerate(rows)]
    v2, i2 = _first_argmax(rest)
    return v1, i1, v2, i2


def _outproj_body(oa_ref, ob_ref, oc_ref, wo_ref, x_ref, g_ref, b_ref, rwt_ref, rb_ref,
                  x1_ref, route_ref, cnt_ref):
    i = pl.program_id(0)
    tm = x_ref.shape[0]

    @pl.when(i == 0)
    def _():
        cnt_ref[...] = jnp.zeros_like(cnt_ref)

    mix = (jnp.dot(oa_ref[...], wo_ref[0:MLA_W, :], preferred_element_type=F32)
           + jnp.dot(ob_ref[...], wo_ref[MLA_W:MLA_W + MOBA_W, :], preferred_element_type=F32)
           + jnp.dot(oc_ref[...], wo_ref[MLA_W + MOBA_W:, :], preferred_element_type=F32))
    x1 = _layer_norm(ALPHA * x_ref[...] + mix, g_ref[...], b_ref[...])
    x1_ref[...] = x1

    logits = lax.dot_general(rwt_ref[...], x1, (((1,), (1,)), ((), ())),
                             precision=lax.Precision.HIGHEST, preferred_element_type=F32)
    s = jax.nn.sigmoid(logits)
    sb = s + rb_ref[...]
    s_rows = [s[e:e + 1, :] for e in range(N_EXPERTS)]
    sb_rows = [sb[e:e + 1, :] for e in range(N_EXPERTS)]
    grp = lambda rows, g: rows[g * EXP_PER_GROUP:(g + 1) * EXP_PER_GROUP]
    g_scores = []
    for g in range(N_GROUPS):
        v1, _, v2, _ = _top2(grp(sb_rows, g))
        g_scores.append(v1 + v2)
    _, g_sel = _first_argmax(g_scores)
    in_b = [_pick([grp(sb_rows, g)[j] for g in range(N_GROUPS)], g_sel) for j in range(EXP_PER_GROUP)]
    in_s = [_pick([grp(s_rows, g)[j] for g in range(N_GROUPS)], g_sel) for j in range(EXP_PER_GROUP)]
    _, j1, _, j2 = _top2(in_b)
    s1, s2 = _pick(in_s, j1), _pick(in_s, j2)
    den = s1 + s2
    e1 = g_sel * EXP_PER_GROUP + j1
    e2 = g_sel * EXP_PER_GROUP + j2
    route_ref[...] = jnp.concatenate(
        [e1.astype(F32), e2.astype(F32), s1 / den, s2 / den, jnp.zeros((4, tm), F32)], axis=0)
    eio = lax.broadcasted_iota(jnp.int32, (N_EXPERTS, tm), 0)
    hits = jnp.where(eio == e1, 1.0, 0.0) + jnp.where(eio == e2, 1.0, 0.0)
    cnt_ref[...] += jnp.sum(hits, axis=1, keepdims=True)


def _outproj_ln_router(o_a, o_b, o_c, w_out_b, x2, ln_g, ln_b, rwt, rb):
    t = x2.shape[0]
    tm = TM_PROJ
    row = lambda w: pl.BlockSpec((tm, w), lambda i: (i, 0))
    full = lambda a: pl.BlockSpec(a.shape, lambda i: (0,) * a.ndim)
    return pl.pallas_call(
        _outproj_body,
        grid=(t // tm,),
        in_specs=[row(MLA_W), row(MOBA_W), row(DIL_W), full(w_out_b), row(D_MODEL), full(ln_g), full(ln_b),
                  full(rwt), full(rb)],
        out_specs=[row(D_MODEL), pl.BlockSpec((8, tm), lambda i: (0, i)),
                   pl.BlockSpec((N_EXPERTS, LANES), lambda i: (0, 0))],
        out_shape=[jax.ShapeDtypeStruct((t, D_MODEL), F32), jax.ShapeDtypeStruct((8, t), F32),
                   jax.ShapeDtypeStruct((N_EXPERTS, LANES), F32)],
        compiler_params=_cparams("arbitrary"),
        name="outproj_ln_router",
    )(o_a, o_b, o_c, w_out_b, x2, ln_g, ln_b, rwt, rb)


TN_SORT = 512
TM_ROWS = 256


def _dest_body(route_ref, pstart_ref, dest_ref, carry_ref):
    n = route_ref.shape[1]

    @pl.when(pl.program_id(0) == 0)
    def _():
        carry_ref[...] = jnp.broadcast_to(pstart_ref[...], carry_ref.shape)

    before = (lax.broadcasted_iota(jnp.int32, (n, n), 0) < lax.broadcasted_iota(jnp.int32, (n, n), 1))
    before = jnp.where(before, 1.0, 0.0).astype(BF16)
    eio = lax.broadcasted_iota(jnp.int32, (N_EXPERTS, n), 0)
    rows = []
    for k in range(TOP_K):
        e_k = route_ref[k:k + 1, :].astype(jnp.int32)
        hit = jnp.where(eio == e_k, 1.0, 0.0)
        earlier = jnp.dot(hit.astype(BF16), before, preferred_element_type=F32)
        carry = carry_ref[:, 0:1]
        rows.append(jnp.sum(hit * (earlier + carry), axis=0, keepdims=True))
        carry_ref[...] += jnp.sum(hit, axis=1, keepdims=True)
    dest_ref[...] = jnp.concatenate(rows, axis=0).astype(jnp.int32)


def _dest_rows(route, pstart):
    t = route.shape[1]
    return pl.pallas_call(
        _dest_body,
        grid=(t // TN_SORT,),
        in_specs=[pl.BlockSpec((8, TN_SORT), lambda i: (0, i)),
                  pl.BlockSpec((N_EXPERTS, 1), lambda i: (0, 0))],
        out_specs=pl.BlockSpec((TOP_K, TN_SORT), lambda i: (0, i)),
        out_shape=jax.ShapeDtypeStruct((TOP_K, t), jnp.int32),
        scratch_shapes=[pltpu.VMEM((N_EXPERTS, LANES), F32)],
        compiler_params=_cparams("arbitrary"),
        name="moe_dest",
    )(route, pstart)


def _scatter_body(dest_ref, x_ref, xs_in_ref, xs_ref, sem):
    del xs_in_ref
    tm = x_ref.shape[0]

    def row_copy(r, d):
        return pltpu.make_async_copy(x_ref.at[pl.ds(r, 1)], xs_ref.at[pl.ds(d, 1)], sem)

    def issue(r, _):
        for k in range(TOP_K):
            row_copy(r, dest_ref[k, r]).start()
        return 0

    def drain(r, _):
        for k in range(TOP_K):
            row_copy(r, dest_ref[k, r]).wait()
        return 0

    lax.fori_loop(0, tm, issue, 0)
    lax.fori_loop(0, tm, drain, 0)


def _scatter_rows(dest, x1, n_rows):
    t, d = x1.shape
    xs_init = jnp.zeros((n_rows, d), x1.dtype)
    return pl.pallas_call(
        _scatter_body,
        grid=(t // TM_ROWS,),
        in_specs=[pl.BlockSpec((TOP_K, TM_ROWS), lambda i: (0, i), memory_space=pltpu.SMEM),
                  pl.BlockSpec((TM_ROWS, d), lambda i: (i, 0)),
                  pl.BlockSpec(memory_space=pl.ANY)],
        out_specs=pl.BlockSpec(memory_space=pl.ANY),
        out_shape=jax.ShapeDtypeStruct((n_rows, d), x1.dtype),
        scratch_shapes=[pltpu.SemaphoreType.DMA(())],
        input_output_aliases={2: 0},
        compiler_params=_cparams("arbitrary"),
        name="moe_scatter",
    )(dest, x1, xs_init)


def _expert_body(blk_exp_ref, nused_ref, xs_ref, w1_ref, w3_ref, w2_ref, y_ref):
    del blk_exp_ref
    i = pl.program_id(0)

    @pl.when(i < nused_ref[0])
    def _():
        xb = xs_ref[...].astype(BF16)
        h1 = jnp.dot(xb, w1_ref[...].astype(BF16), preferred_element_type=F32)
        h3 = jnp.dot(xb, w3_ref[...].astype(BF16), preferred_element_type=F32)
        hb = (jax.nn.silu(h1) * h3).astype(BF16)
        y_ref[...] = jnp.dot(hb, w2_ref[...].astype(BF16), preferred_element_type=F32)

    @pl.when(i >= nused_ref[0])
    def _():
        y_ref[...] = jnp.zeros_like(y_ref)


def _expert_ffn(blk_exp, nused, xs, w1, w3, w2):
    n_rows, d = xs.shape
    n_blk = n_rows // MOE_BLOCK
    grid_spec = pltpu.PrefetchScalarGridSpec(
        num_scalar_prefetch=2,
        grid=(n_blk,),
        in_specs=[pl.BlockSpec((MOE_BLOCK, d), lambda i, be, nu: (i, 0)),
                  pl.BlockSpec((None, d, D_EXPERT), lambda i, be, nu: (be[i], 0, 0)),
                  pl.BlockSpec((None, d, D_EXPERT), lambda i, be, nu: (be[i], 0, 0)),
                  pl.BlockSpec((None, D_EXPERT, d), lambda i, be, nu: (be[i], 0, 0))],
        out_specs=pl.BlockSpec((MOE_BLOCK, d), lambda i, be, nu: (i, 0)),
    )
    return pl.pallas_call(
        _expert_body,
        grid_spec=grid_spec,
        out_shape=jax.ShapeDtypeStruct((n_rows, d), F32),
        compiler_params=_cparams("arbitrary"),
        name="moe_experts",
    )(blk_exp, nused, xs, w1, w3, w2)


def _combine_body(dest_ref, route_ref, x1_ref, g_ref, b_ref, ys_ref, o_ref, buf_ref, sem):
    tm = x1_ref.shape[0]

    def row_copy(k, r):
        return pltpu.make_async_copy(ys_ref.at[pl.ds(dest_ref[k, r], 1)], buf_ref.at[k, pl.ds(r, 1)], sem)

    def issue(r, _):
        for k in range(TOP_K):
            row_copy(k, r).start()
        return 0

    def drain(r, _):
        for k in range(TOP_K):
            row_copy(k, r).wait()
        return 0

    lax.fori_loop(0, tm, issue, 0)
    lax.fori_loop(0, tm, drain, 0)
    ffn = jnp.zeros((tm, D_MODEL), F32)
    for k in range(TOP_K):
        gate = jnp.broadcast_to(route_ref[TOP_K + k:TOP_K + k + 1, :], (LANES, tm)).T
        ffn = ffn + jnp.tile(gate, (1, D_MODEL // LANES)) * buf_ref[k]
    o_ref[...] = _layer_norm(ALPHA * x1_ref[...] + ffn, g_ref[...], b_ref[...])


def _combine_ln(dest, route, x1, ln_g, ln_b, ys):
    t, d = x1.shape
    full = lambda a: pl.BlockSpec(a.shape, lambda i: (0,) * a.ndim)
    return pl.pallas_call(
        _combine_body,
        grid=(t // TM_ROWS,),
        in_specs=[pl.BlockSpec((TOP_K, TM_ROWS), lambda i: (0, i), memory_space=pltpu.SMEM),
                  pl.BlockSpec((8, TM_ROWS), lambda i: (0, i)),
                  pl.BlockSpec((TM_ROWS, d), lambda i: (i, 0)), full(ln_g), full(ln_b),
                  pl.BlockSpec(memory_space=pl.ANY)],
        out_specs=pl.BlockSpec((TM_ROWS, d), lambda i: (i, 0)),
        out_shape=jax.ShapeDtypeStruct((t, d), F32),
        scratch_shapes=[pltpu.VMEM((TOP_K, TM_ROWS, d), F32), pltpu.SemaphoreType.DMA(())],
        compiler_params=_cparams("arbitrary"),
        name="moe_combine_ln",
    )(dest, route, x1, ln_g, ln_b, ys)


def _moe_layer(x1, route, counts, w1, w3, w2, ln_g, ln_b):
    t = x1.shape[0]
    n_rows = t * TOP_K + N_EXPERTS * MOE_BLOCK
    n_blk = n_rows // MOE_BLOCK
    cnt = counts[:, 0].astype(jnp.int32)
    padded = (cnt + MOE_BLOCK - 1) // MOE_BLOCK * MOE_BLOCK
    pend = jnp.cumsum(padded)
    pstart = (pend - padded).astype(F32).reshape(N_EXPERTS, 1)
    blk_row0 = jnp.arange(n_blk, dtype=jnp.int32) * MOE_BLOCK
    blk_exp = jnp.minimum(jnp.sum((pend[None, :] <= blk_row0[:, None]).astype(jnp.int32), axis=1), N_EXPERTS - 1)
    nused = (pend[-1:] // MOE_BLOCK).astype(jnp.int32)
    dest = _dest_rows(route, pstart)
    xs = _scatter_rows(dest, x1, n_rows)
    ys = _expert_ffn(blk_exp, nused, xs, w1, w3, w2)
    return _combine_ln(dest, route, x1, ln_g, ln_b, ys)


def _rope_tables(seq):
    def tab(dim):
        inv = ROPE_THETA ** (-jnp.arange(0, dim, 2, dtype=F32) / dim)
        ang = jnp.arange(seq, dtype=F32)[:, None] * inv[None, :]
        return jnp.cos(ang), jnp.sin(ang)

    c, s = tab(HEAD_DIM)
    cos64 = jnp.tile(jnp.concatenate([c, c], axis=1), (1, LANES // HEAD_DIM))
    sin64 = jnp.tile(jnp.concatenate([-s, s], axis=1), (1, LANES // HEAD_DIM))
    c, s = tab(MLA_ROPE)
    pad_l, pad_r = MLA_ROPE_LANE, LANES - MLA_ROPE_LANE - MLA_ROPE
    cosm = jnp.concatenate([jnp.ones((seq, pad_l), F32), c, c, jnp.ones((seq, pad_r), F32)], axis=1)
    sinm = jnp.concatenate([jnp.zeros((seq, pad_l), F32), -s, s, jnp.zeros((seq, pad_r), F32)], axis=1)
    return cos64, sin64, cosm, sinm


def _prep_mixer_weights(w_in, q_norm, w_uq, kv_norm, w_ukv):
    d = w_in.shape[0]
    lat = MLA_Q_LORA + MLA_KV_LORA
    w_in_p = jnp.concatenate([
        w_in[:, :lat], jnp.zeros((d, MLA_ROPE_LANE), F32), w_in[:, lat:lat + MLA_ROPE],
        jnp.zeros((d, LANES - MLA_ROPE_LANE - MLA_ROPE), F32), w_in[:, lat + MLA_ROPE:]], axis=1).astype(BF16)
    wq = w_uq.reshape(MLA_Q_LORA, MLA_HEADS, MLA_NOPE + MLA_ROPE)
    wuq_p = jnp.pad(wq, ((0, 0), (0, 0), (0, MLA_HEAD_PAD - MLA_NOPE - MLA_ROPE)))
    wuq_p = wuq_p.reshape(MLA_Q_LORA, MLA_HEADS * MLA_HEAD_PAD).astype(BF16)
    wkv = w_ukv.reshape(MLA_KV_LORA, MLA_HEADS, MLA_NOPE + MLA_V)
    wk = jnp.pad(wkv[:, :, :MLA_NOPE], ((0, 0), (0, 0), (0, MLA_HEAD_PAD - MLA_NOPE)))
    wukv_p = jnp.concatenate([wk.reshape(MLA_KV_LORA, MLA_HEADS * MLA_HEAD_PAD),
                              wkv[:, :, MLA_NOPE:].reshape(MLA_KV_LORA, MLA_W)], axis=1).astype(BF16)
    return w_in_p, q_norm.reshape(1, -1), wuq_p, kv_norm.reshape(1, -1), wukv_p


def _mixer_heads(x2, w_in, q_norm, w_uq, kv_norm, w_ukv, tabs, batch, seq):
    w_in_p, qn, wuq_p, kvn, wukv_p = _prep_mixer_weights(w_in, q_norm, w_uq, kv_norm, w_ukv)
    qm, km, vm, qb, kb, vb, qc, kc, vc = _inproj(x2, w_in_p, qn, kvn, wuq_p, wukv_p, tabs, seq)
    o_a = _flash_attention(qm, km, vm, MLA_HEADS, batch, seq, "mla_attn")
    o_b = _flash_attention(qb, kb, vb, MOBA_HEADS, batch, seq, "moba_attn")
    o_c = _dil_attention(qc, kc, vc, batch, seq)
    return o_a, o_b, o_c


def kernel(x, w_in, mla_q_norm, mla_w_uq, mla_kv_norm, mla_w_ukv, w_out, ln1_g, ln1_b, router_w, router_b, moe_w1, moe_w3, moe_w2, ln2_g, ln2_b):
    batch, seq, d = x.shape
    tabs = _rope_tables(seq)
    x2 = x.reshape(batch * seq, d)
    rwt = router_w.T
    rb = router_b.reshape(N_EXPERTS, 1)
    for l in range(DEPTH):
        o_a, o_b, o_c = _mixer_heads(x2, w_in[l], mla_q_norm[l], mla_w_uq[l], mla_kv_norm[l], mla_w_ukv[l],
                                     tabs, batch, seq)
        x1, route, counts = _outproj_ln_router(o_a, o_b, o_c, w_out[l].astype(BF16), x2,
                                               ln1_g[l].reshape(1, d), ln1_b[l].reshape(1, d), rwt, rb)
        x2 = _moe_layer(x1, route, counts, moe_w1[l], moe_w3[l], moe_w2[l],
                        ln2_g[l].reshape(1, d), ln2_b[l].reshape(1, d))
    return x2.reshape(batch, seq, d)
```

```python
import functools

import jax
import jax.numpy as jnp
import numpy as np
from jax import lax
from jax.experimental import pallas as pl
from jax.experimental.pallas import tpu as pltpu

F32 = jnp.float32
BF16 = jnp.bfloat16
LANES = 128
NEG_INF = float("-inf")
VMEM_LIMIT_BYTES = 56 * 1024 * 1024

D_MODEL = 1024
DEPTH = 2
HEAD_DIM = 64
ROPE_THETA = 10000.0
MLA_HEADS = 4
MLA_NOPE = 64
MLA_ROPE = 32
MLA_V = 64
MLA_Q_LORA = 256
MLA_KV_LORA = 128
MOBA_HEADS = 6
MOBA_BLOCK = 256
MOBA_TOPK = 3
DIL_HEADS = 6
DIL_PAIRS = ((128, 1), (512, 4), (2048, 16))
MLA_W = MLA_HEADS * MLA_V
MOBA_W = MOBA_HEADS * HEAD_DIM
DIL_W = DIL_HEADS * HEAD_DIM
N_EXPERTS = 64
N_GROUPS = 8
EXP_PER_GROUP = N_EXPERTS // N_GROUPS
TOP_K = 2
D_EXPERT = 256
MOE_BLOCK = 256
ALPHA = (2 * DEPTH) ** 0.25
LN_EPS = 1e-5
RMS_EPS = 1e-6

SEG_A = 512
SEG_B = 3 * MOBA_W
SEG_C = 3 * DIL_W
N_IN_PAD = SEG_A + SEG_B + SEG_C
MLA_HEAD_PAD = 128
MLA_ROPE_LANE = 64

TM_PROJ = 512
TQ = 256
KB_WIDE = 4
DIL_W_UNITS = 128
LOG2_E = 1.4426950408889634
ONES_ROW = HEAD_DIM
SEL_LANE = HEAD_DIM
MASK_BIG = 2.0 ** 100


def _cparams(*sem):
    return pltpu.CompilerParams(dimension_semantics=sem, vmem_limit_bytes=VMEM_LIMIT_BYTES)


def _rope(x, cos, sin_signed, half):
    lane = lax.broadcasted_iota(jnp.int32, x.shape, 1)
    first = ((lane // half) % 2) == 0
    rot = jnp.where(first, pltpu.roll(x, LANES - half, 1), pltpu.roll(x, half, 1))
    return x * cos + rot * sin_signed


def _rms(x, g):
    return x * lax.rsqrt(jnp.mean(jnp.square(x), axis=-1, keepdims=True) + RMS_EPS) * g


def _inproj_body(seq_blocks, x_ref, w_ref, qn_ref, kvn_ref, wuq_ref, wukv_ref,
                 cos64_ref, sin64_ref, cosm_ref, sinm_ref,
                 qm_ref, km_ref, vm_ref, qb_ref, kb_ref, vb_ref, qc_ref, kc_ref, vc_ref,
                 kmean_ref):
    i = pl.program_id(0)
    tm = x_ref.shape[0]
    n_kb = kmean_ref.shape[0]
    sblk = i % seq_blocks

    @pl.when(sblk == 0)
    def _():
        kmean_ref[...] = jnp.zeros_like(kmean_ref)

    xb = x_ref[...].astype(BF16)
    cos64, sin64 = cos64_ref[...], sin64_ref[...]
    cosm, sinm = cosm_ref[...], sinm_ref[...]

    h_a = jnp.dot(xb, w_ref[:, 0:SEG_A], preferred_element_type=F32)
    cq = _rms(h_a[:, 0:MLA_Q_LORA], qn_ref[...]).astype(BF16)
    ckv = _rms(h_a[:, MLA_Q_LORA:MLA_Q_LORA + MLA_KV_LORA], kvn_ref[...]).astype(BF16)
    kr = _rope(h_a[:, MLA_Q_LORA + MLA_KV_LORA:SEG_A], cosm, sinm, MLA_ROPE // 2)
    q_m = jnp.dot(cq, wuq_ref[...], preferred_element_type=F32)
    kv_m = jnp.dot(ckv, wukv_ref[...], preferred_element_type=F32)
    mla_scale = (MLA_NOPE + MLA_ROPE) ** -0.5 * LOG2_E
    lane_t = lax.broadcasted_iota(jnp.int32, (tm, LANES), 1)
    row_t = lax.broadcasted_iota(jnp.int32, (LANES, tm), 0)
    zeros_half = jnp.zeros((LANES - HEAD_DIM, tm), F32)

    def put_t(ref, h, val_t):
        for b in range(tm // TQ):
            ref[b, h * LANES:(h + 1) * LANES, :] = val_t[:, b * TQ:(b + 1) * TQ].astype(ref.dtype)

    def v_heads_t(pair):
        pt = pair.T
        return [jnp.where(row_t == ONES_ROW, 1.0, jnp.concatenate([pt[s * HEAD_DIM:(s + 1) * HEAD_DIM], zeros_half], 0))
                for s in range(2)]

    for h in range(MLA_HEADS):
        sl = slice(h * MLA_HEAD_PAD, (h + 1) * MLA_HEAD_PAD)
        put_t(qm_ref, h, (_rope(q_m[:, sl], cosm, sinm, MLA_ROPE // 2) * mla_scale).T)
        km_ref[:, sl] = (kv_m[:, sl] + kr).astype(BF16)
    for c in range(MLA_HEADS // 2):
        v_pair = kv_m[:, (MLA_HEADS + c) * LANES:(MLA_HEADS + c + 1) * LANES]
        for s, v_t in enumerate(v_heads_t(v_pair)):
            put_t(vm_ref, 2 * c + s, v_t)

    h_b = jnp.dot(xb, w_ref[:, SEG_A:SEG_A + SEG_B], preferred_element_type=F32)
    scale = HEAD_DIM ** -0.5
    lane = lax.broadcasted_iota(jnp.int32, (n_kb, LANES), 1)
    pos = sblk * tm + lax.broadcasted_iota(jnp.int32, (1, tm), 1)
    q_blk = pos // MOBA_BLOCK
    jio = lax.broadcasted_iota(jnp.int32, (n_kb, tm), 0)
    elig = jio < q_blk
    row_blk = (sblk * tm + lax.broadcasted_iota(jnp.int32, (tm, LANES), 0)) // MOBA_BLOCK
    blk_lane = lane_t == SEL_LANE + row_blk
    first_head = lane_t < HEAD_DIM

    def split(pair):
        return (jnp.where(first_head, pair, 0.0), jnp.where(first_head, pltpu.roll(pair, HEAD_DIM, 1), 0.0))

    for c in range(MOBA_W // LANES):
        sl = slice(c * LANES, (c + 1) * LANES)
        q = _rope(h_b[:, sl], cos64, sin64, HEAD_DIM // 2)
        k = _rope(h_b[:, MOBA_W + c * LANES:MOBA_W + (c + 1) * LANES], cos64, sin64, HEAD_DIM // 2)
        km = kmean_ref[:, sl]
        kb_row = lax.broadcasted_iota(jnp.int32, (n_kb, LANES), 0)
        for b in range(tm // MOBA_BLOCK):
            mean_b = jnp.mean(k[b * MOBA_BLOCK:(b + 1) * MOBA_BLOCK], axis=0, keepdims=True)
            km = jnp.where(kb_row == sblk * (tm // MOBA_BLOCK) + b, mean_b, km)
        kmean_ref[:, sl] = km
        q_t = (q * (scale * LOG2_E)).T
        k_heads = split(k)
        v_t = v_heads_t(h_b[:, 2 * MOBA_W + c * LANES:2 * MOBA_W + (c + 1) * LANES])
        for half in range(2):
            hs = slice((2 * c + half) * LANES, (2 * c + half + 1) * LANES)
            in_head = (lane // HEAD_DIM) == half
            gate = lax.dot_general(jnp.where(in_head, km, 0.0), q, (((1,), (1,)), ((), ())),
                                   precision=lax.Precision.HIGHEST,
                                   preferred_element_type=F32)
            gate = jnp.where(elig, gate, NEG_INF)
            q_rows = [q_t[half * HEAD_DIM:(half + 1) * HEAD_DIM]]
            for j in range(n_kb):
                gj = gate[j:j + 1, :]
                beats = (gate > gj) | ((gate == gj) & (jio < j))
                cnt = jnp.sum(beats.astype(F32), axis=0, keepdims=True)
                keep = ((cnt < MOBA_TOPK) & elig[j:j + 1, :]) | (q_blk == j)
                q_rows.append(jnp.where(keep, 0.0, -MASK_BIG))
            q_rows.append(jnp.zeros((LANES - SEL_LANE - n_kb, tm), F32))
            put_t(qb_ref, 2 * c + half, jnp.concatenate(q_rows, axis=0))
            kb_ref[:, hs] = jnp.where(blk_lane, 1.0, k_heads[half]).astype(BF16)
            put_t(vb_ref, 2 * c + half, v_t[half])

    h_c = jnp.dot(xb, w_ref[:, SEG_A + SEG_B:], preferred_element_type=F32)
    for c in range(DIL_W // LANES):
        sl = slice(c * LANES, (c + 1) * LANES)
        qc_ref[:, sl] = _rope(h_c[:, sl], cos64, sin64, HEAD_DIM // 2) * (scale * LOG2_E)
        kc_ref[:, sl] = _rope(h_c[:, DIL_W + c * LANES:DIL_W + (c + 1) * LANES], cos64, sin64, HEAD_DIM // 2)
    vc_ref[...] = h_c[:, 2 * DIL_W:]


def _inproj(x2, w_in_p, qn, kvn, wuq_p, wukv_p, tabs, seq):
    t = x2.shape[0]
    tm = TM_PROJ
    seq_blocks = seq // tm
    n_kb = seq // MOBA_BLOCK
    row = lambda w: pl.BlockSpec((tm, w), lambda i: (i, 0))
    full = lambda a: pl.BlockSpec(a.shape, lambda i: (0,) * a.ndim)
    tab = pl.BlockSpec((tm, LANES), lambda i: (i % seq_blocks, 0))
    out_shapes = [
        jax.ShapeDtypeStruct((t // TQ, MLA_HEADS * LANES, TQ), BF16),
        jax.ShapeDtypeStruct((t, MLA_HEADS * MLA_HEAD_PAD), BF16),
        jax.ShapeDtypeStruct((t // TQ, MLA_HEADS * LANES, TQ), BF16),
        jax.ShapeDtypeStruct((t // TQ, MOBA_HEADS * LANES, TQ), BF16),
        jax.ShapeDtypeStruct((t, MOBA_HEADS * LANES), BF16),
        jax.ShapeDtypeStruct((t // TQ, MOBA_HEADS * LANES, TQ), BF16),
        jax.ShapeDtypeStruct((t, DIL_W), F32),
        jax.ShapeDtypeStruct((t, DIL_W), F32),
        jax.ShapeDtypeStruct((t, DIL_W), F32),
    ]
    return pl.pallas_call(
        functools.partial(_inproj_body, seq_blocks),
        grid=(t // tm,),
        in_specs=[row(D_MODEL), full(w_in_p), full(qn), full(kvn), full(wuq_p), full(wukv_p),
                  tab, tab, tab, tab],
        out_specs=[row(s.shape[1]) if len(s.shape) == 2 else
                   pl.BlockSpec((tm // TQ,) + s.shape[1:], lambda i: (i, 0, 0)) for s in out_shapes],
        out_shape=out_shapes,
        scratch_shapes=[pltpu.VMEM((n_kb, MOBA_W), F32)],
        compiler_params=_cparams("arbitrary"),
        name="inproj",
    )(x2, w_in_p, qn, kvn, wuq_p, wukv_p, *tabs)


def _flash_body(n_heads, qt_ref, k_ref, vt_ref, o_ref, *state):
    i = pl.program_id(1)
    tq = qt_ref.shape[1]

    def head(h):
        return slice(h * LANES, (h + 1) * LANES)

    m_refs, acc_refs = state[:n_heads], state[n_heads:]
    key_le_query = (lax.broadcasted_iota(jnp.int32, (tq, tq), 0) <= lax.broadcasted_iota(jnp.int32, (tq, tq), 1))

    def scores(j, n, h):
        rows = pl.ds(pl.multiple_of(j * tq, tq), n * tq)
        return jnp.dot(k_ref[rows, head(h)], qt_ref[head(h), :], preferred_element_type=F32)

    def pv(j, n, h, p):
        return sum(jnp.dot(vt_ref[j + b, head(h), :], p[b * tq:(b + 1) * tq], preferred_element_type=F32)
                   for b in range(n))

    def own_block(h, s):
        s = jnp.where(key_le_query, s, NEG_INF)
        m = jnp.max(s, axis=0, keepdims=True)
        acc_refs[h][...] = pv(i, 1, h, jnp.exp2(s - m).astype(BF16))
        m_refs[h][...] = m

    def past_blocks(j, n, h, s):
        m_old = m_refs[h][...]
        m_new = jnp.maximum(m_old, jnp.max(s, axis=0, keepdims=True))
        p = jnp.exp2(s - m_new).astype(BF16)
        acc_refs[h][...] = jnp.exp2(m_old - m_new) * acc_refs[h][...] + pv(j, n, h, p)
        m_refs[h][...] = m_new

    def sweep(j, n, update):
        s_next = scores(j, n, 0)
        for h in range(n_heads):
            s = s_next
            if h + 1 < n_heads:
                s_next = scores(j, n, h + 1)
            update(h, s)

    sweep(i, 1, own_block)

    def wide_body(jj, _):
        sweep(jj * KB_WIDE, KB_WIDE, functools.partial(past_blocks, jj * KB_WIDE, KB_WIDE))
        return 0

    def narrow_body(j, _):
        sweep(j, 1, functools.partial(past_blocks, j, 1))
        return 0

    n_wide = i // KB_WIDE
    lax.fori_loop(0, n_wide, wide_body, 0)
    lax.fori_loop(n_wide * KB_WIDE, i, narrow_body, 0)

    for c in range(n_heads // 2):
        halves = []
        for h in (2 * c, 2 * c + 1):
            acc = acc_refs[h][...]
            halves.append(acc[0:HEAD_DIM] / acc[ONES_ROW:ONES_ROW + 1])
        o_ref[:, head(c)] = jnp.concatenate(halves, axis=0).T.astype(o_ref.dtype)


def _flash_attention(qt, k, vt, n_heads, batch, seq, name):
    nq = seq // TQ
    width = n_heads * LANES
    return pl.pallas_call(
        functools.partial(_flash_body, n_heads),
        grid=(batch, nq),
        in_specs=[pl.BlockSpec((None, width, TQ), lambda b, i: (b * nq + i, 0, 0)),
                  pl.BlockSpec((seq, width), lambda b, i: (b, 0)),
                  pl.BlockSpec((nq, width, TQ), lambda b, i: (b, 0, 0))],
        out_specs=pl.BlockSpec((TQ, n_heads * HEAD_DIM), lambda b, i: (b * nq + i, 0)),
        out_shape=jax.ShapeDtypeStruct((batch * seq, n_heads * HEAD_DIM), BF16),
        scratch_shapes=[pltpu.VMEM((1, TQ), F32)] * n_heads + [pltpu.VMEM((LANES, TQ), F32)] * n_heads,
        compiler_params=_cparams("parallel", "arbitrary"),
        name=name,
    )(qt, k, vt)


def _dil_body(seq, q_ref, k_ref, v_ref, o_ref, ob_ref, lse_ref, ks_ref, vt_ref):
    w = DIL_W_UNITS
    n_all = seq // w
    frow = lax.broadcasted_iota(jnp.int32, (LANES, w), 0)
    key = lax.broadcasted_iota(jnp.int32, (2 * w, w), 0)
    qry = lax.broadcasted_iota(jnp.int32, (2 * w, w), 1)
    cur_ok = (key >= w) & (key - w <= qry)
    prev_ok = (key < w) & (key >= qry)
    zeros_half = jnp.zeros((LANES - HEAD_DIM, w), F32)
    ks_ref[0:w, :] = jnp.zeros((w, LANES), BF16)
    for h in range(2):
        vt_ref[h, 0] = jnp.zeros((LANES, w), BF16)
    for bi, (_, dil) in enumerate(DIL_PAIRS):
        n_blk = seq // (dil * w)

        def rows_of(idx, dil=dil, n_blk=n_blk):
            start = idx // n_blk + (idx % n_blk) * (w * dil)
            return pl.ds(start, w, stride=dil) if dil > 1 else pl.ds(pl.multiple_of(start, w), w)

        def stage(idx, _, rows_of=rows_of):
            rows = rows_of(idx)
            ks_ref[pl.ds(pl.multiple_of((idx + 1) * w, w), w), :] = k_ref[rows, :].astype(BF16)
            v_t = v_ref[rows, :].T
            for h in range(2):
                v_h = jnp.concatenate([v_t[h * HEAD_DIM:(h + 1) * HEAD_DIM], zeros_half], axis=0)
                vt_ref[h, idx + 1] = jnp.where(frow == ONES_ROW, 1.0, v_h).astype(BF16)
            return 0

        lax.fori_loop(0, n_all, stage, 0, unroll=2)

        def one(idx, _, bi=bi, n_blk=n_blk, rows_of=rows_of):
            rows = rows_of(idx)
            q_t = q_ref[rows, :].T
            keys = ks_ref[pl.ds(pl.multiple_of(idx * w, w), 2 * w), :]
            visible = cur_ok | (prev_ok & (idx % n_blk > 0))
            outs, lses = [], []
            for h in range(2):
                q_h = jnp.where((frow // HEAD_DIM) == h, q_t, 0.0).astype(BF16)
                s = jnp.where(visible, jnp.dot(keys, q_h, preferred_element_type=F32), NEG_INF)
                m = jnp.max(s, axis=0, keepdims=True)
                p = jnp.exp2(s - m).astype(BF16)
                acc = (jnp.dot(vt_ref[h, idx], p[0:w], preferred_element_type=F32)
                       + jnp.dot(vt_ref[h, idx + 1], p[w:2 * w], preferred_element_type=F32))
                l = acc[ONES_ROW:ONES_ROW + 1]
                outs.append(acc[0:HEAD_DIM] / l)
                lses.append(jnp.broadcast_to(m + jnp.log2(l), (HEAD_DIM, w)))
            ob_ref[bi, rows, :] = jnp.concatenate(outs, axis=0).T
            lse_ref[bi, rows, :] = jnp.concatenate(lses, axis=0).T
            return 0

        lax.fori_loop(0, n_all, one, 0, unroll=8)

    def merge(c, _):
        rows = pl.ds(pl.multiple_of(c * TQ, TQ), TQ)
        ls = [lse_ref[b, rows, :] for b in range(len(DIL_PAIRS))]
        top = functools.reduce(jnp.maximum, ls)
        ws = [jnp.exp2(x - top) for x in ls]
        num = sum(wb * ob_ref[b, rows, :] for b, wb in enumerate(ws))
        o_ref[rows, :] = (num / sum(ws)).astype(o_ref.dtype)
        return 0

    lax.fori_loop(0, seq // TQ, merge, 0)


def _dil_attention(qc, kc, vc, batch, seq):
    blk = pl.BlockSpec((seq, LANES), lambda b, g: (b, g))
    nbr = len(DIL_PAIRS)
    return pl.pallas_call(
        functools.partial(_dil_body, seq),
        grid=(batch, DIL_HEADS // 2),
        in_specs=[blk, blk, blk],
        out_specs=blk,
        out_shape=jax.ShapeDtypeStruct((batch * seq, DIL_W), BF16),
        scratch_shapes=[pltpu.VMEM((nbr, seq, LANES), F32), pltpu.VMEM((nbr, seq, LANES), F32),
                        pltpu.VMEM((seq + DIL_W_UNITS, LANES), BF16),
                        pltpu.VMEM((2, seq // DIL_W_UNITS + 1, LANES, DIL_W_UNITS), BF16)],
        compiler_params=_cparams("parallel", "parallel"),
        name="dil_attn",
    )(qc, kc, vc)


def _layer_norm(y, g, b):
    mu = jnp.mean(y, axis=-1, keepdims=True)
    yc = y - mu
    var = jnp.mean(jnp.square(yc), axis=-1, keepdims=True)
    return yc * lax.rsqrt(var + LN_EPS) * g + b


def _first_argmax(rows):
    best, idx = rows[0], jnp.zeros(rows[0].shape, jnp.int32)
    for j in range(1, len(rows)):
        upd = rows[j] > best
        idx = jnp.where(upd, j, idx)
        best = jnp.where(upd, rows[j], best)
    return best, idx


def _pick(rows, idx):
    out = rows[0]
    for j in range(1, len(rows)):
        out = jnp.where(idx == j, rows[j], out)
    return out


def _top2(rows):
    v1, i1 = _first_argmax(rows)
    rest = [jnp.where(i1 == j, NEG_INF, r) for j, r in enumerate(rows)]
    v2, i2 = _first_argmax(rest)
    return v1, i1, v2, i2


def _outproj_body(oa_ref, ob_ref, oc_ref, wo_ref, x_ref, g_ref, b_ref, rwt_ref, rb_ref,
                  x1_ref, route_ref, cnt_ref):
    i = pl.program_id(0)
    tm = x_ref.shape[0]

    @pl.when(i == 0)
    def _():
        cnt_ref[...] = jnp.zeros_like(cnt_ref)

    mix = (jnp.dot(oa_ref[...], wo_ref[0:MLA_W, :], preferred_element_type=F32)
           + jnp.dot(ob_ref[...], wo_ref[MLA_W:MLA_W + MOBA_W, :], preferred_element_type=F32)
           + jnp.dot(oc_ref[...], wo_ref[MLA_W + MOBA_W:, :], preferred_element_type=F32))
    x1 = _layer_norm(ALPHA * x_ref[...] + mix, g_ref[...], b_ref[...])
    x1_ref[...] = x1

    logits = lax.dot_general(rwt_ref[...], x1, (((1,), (1,)), ((), ())),
                             precision=lax.Precision.HIGHEST, preferred_element_type=F32)
    s = jax.nn.sigmoid(logits)
    sb = s + rb_ref[...]
    s_rows = [s[e:e + 1, :] for e in range(N_EXPERTS)]
    sb_rows = [sb[e:e + 1, :] for e in range(N_EXPERTS)]
    grp = lambda rows, g: rows[g * EXP_PER_GROUP:(g + 1) * EXP_PER_GROUP]
    g_scores = []
    for g in range(N_GROUPS):
        v1, _, v2, _ = _top2(grp(sb_rows, g))
        g_scores.append(v1 + v2)
    _, g_sel = _first_argmax(g_scores)
    in_b = [_pick([grp(sb_rows, g)[j] for g in range(N_GROUPS)], g_sel) for j in range(EXP_PER_GROUP)]
    in_s = [_pick([grp(s_rows, g)[j] for g in range(N_GROUPS)], g_sel) for j in range(EXP_PER_GROUP)]
    _, j1, _, j2 = _top2(in_b)
    s1, s2 = _pick(in_s, j1), _pick(in_s, j2)
    den = s1 + s2
    e1 = g_sel * EXP_PER_GROUP + j1
    e2 = g_sel * EXP_PER_GROUP + j2
    route_ref[...] = jnp.concatenate(
        [e1.astype(F32), e2.astype(F32), s1 / den, s2 / den, jnp.zeros((4, tm), F32)], axis=0)
    eio = lax.broadcasted_iota(jnp.int32, (N_EXPERTS, tm), 0)
    hits = jnp.where(eio == e1, 1.0, 0.0) + jnp.where(eio == e2, 1.0, 0.0)
    cnt_ref[...] += jnp.sum(hits, axis=1, keepdims=True)


def _outproj_ln_router(o_a, o_b, o_c, w_out_b, x2, ln_g, ln_b, rwt, rb):
    t = x2.shape[0]
    tm = TM_PROJ
    row = lambda w: pl.BlockSpec((tm, w), lambda i: (i, 0))
    full = lambda a: pl.BlockSpec(a.shape, lambda i: (0,) * a.ndim)
    return pl.pallas_call(
        _outproj_body,
        grid=(t // tm,),
        in_specs=[row(MLA_W), row(MOBA_W), row(DIL_W), full(w_out_b), row(D_MODEL), full(ln_g), full(ln_b),
                  full(rwt), full(rb)],
        out_specs=[row(D_MODEL), pl.BlockSpec((8, tm), lambda i: (0, i)),
                   pl.BlockSpec((N_EXPERTS, LANES), lambda i: (0, 0))],
        out_shape=[jax.ShapeDtypeStruct((t, D_MODEL), F32), jax.ShapeDtypeStruct((8, t), F32),
                   jax.ShapeDtypeStruct((N_EXPERTS, LANES), F32)],
        compiler_params=_cparams("arbitrary"),
        name="outproj_ln_router",
    )(o_a, o_b, o_c, w_out_b, x2, ln_g, ln_b, rwt, rb)


TN_SORT = 512
TM_ROWS = 256


def _dest_body(route_ref, pstart_ref, dest_ref, carry_ref):
    n = route_ref.shape[1]

    @pl.when(pl.program_id(0) == 0)
    def _():
        carry_ref[...] = jnp.broadcast_to(pstart_ref[...], carry_ref.shape)

    before = (lax.broadcasted_iota(jnp.int32, (n, n), 0) < lax.broadcasted_iota(jnp.int32, (n, n), 1))
    before = jnp.where(before, 1.0, 0.0).astype(BF16)
    eio = lax.broadcasted_iota(jnp.int32, (N_EXPERTS, n), 0)
    rows = []
    for k in range(TOP_K):
        e_k = route_ref[k:k + 1, :].astype(jnp.int32)
        hit = jnp.where(eio == e_k, 1.0, 0.0)
        earlier = jnp.dot(hit.astype(BF16), before, preferred_element_type=F32)
        carry = carry_ref[:, 0:1]
        rows.append(jnp.sum(hit * (earlier + carry), axis=0, keepdims=True))
        carry_ref[...] += jnp.sum(hit, axis=1, keepdims=True)
    dest_ref[...] = jnp.concatenate(rows, axis=0).astype(jnp.int32)


def _dest_rows(route, pstart):
    t = route.shape[1]
    return pl.pallas_call(
        _dest_body,
        grid=(t // TN_SORT,),
        in_specs=[pl.BlockSpec((8, TN_SORT), lambda i: (0, i)),
                  pl.BlockSpec((N_EXPERTS, 1), lambda i: (0, 0))],
        out_specs=pl.BlockSpec((TOP_K, TN_SORT), lambda i: (0, i)),
        out_shape=jax.ShapeDtypeStruct((TOP_K, t), jnp.int32),
        scratch_shapes=[pltpu.VMEM((N_EXPERTS, LANES), F32)],
        compiler_params=_cparams("arbitrary"),
        name="moe_dest",
    )(route, pstart)


def _scatter_body(dest_ref, x_ref, xs_in_ref, xs_ref, sem):
    del xs_in_ref
    tm = x_ref.shape[0]

    def row_copy(r, d):
        return pltpu.make_async_copy(x_ref.at[pl.ds(r, 1)], xs_ref.at[pl.ds(d, 1)], sem)

    def issue(r, _):
        for k in range(TOP_K):
            row_copy(r, dest_ref[k, r]).start()
        return 0

    def drain(r, _):
        for k in range(TOP_K):
            row_copy(r, dest_ref[k, r]).wait()
        return 0

    lax.fori_loop(0, tm, issue, 0)
    lax.fori_loop(0, tm, drain, 0)


def _scatter_rows(dest, x1, n_rows):
    t, d = x1.shape
    xs_init = jnp.zeros((n_rows, d), x1.dtype)
    return pl.pallas_call(
        _scatter_body,
        grid=(t // TM_ROWS,),
        in_specs=[pl.BlockSpec((TOP_K, TM_ROWS), lambda i: (0, i), memory_space=pltpu.SMEM),
                  pl.BlockSpec((TM_ROWS, d), lambda i: (i, 0)),
                  pl.BlockSpec(memory_space=pl.ANY)],
        out_specs=pl.BlockSpec(memory_space=pl.ANY),
        out_shape=jax.ShapeDtypeStruct((n_rows, d), x1.dtype),
        scratch_shapes=[pltpu.SemaphoreType.DMA(())],
        input_output_aliases={2: 0},
        compiler_params=_cparams("arbitrary"),
        name="moe_scatter",
    )(dest, x1, xs_init)


def _expert_body(blk_exp_ref, nused_ref, xs_ref, w1_ref, w3_ref, w2_ref, y_ref):
    del blk_exp_ref
    i = pl.program_id(0)

    @pl.when(i < nused_ref[0])
    def _():
        xb = xs_ref[...].astype(BF16)
        h1 = jnp.dot(xb, w1_ref[...].astype(BF16), preferred_element_type=F32)
        h3 = jnp.dot(xb, w3_ref[...].astype(BF16), preferred_element_type=F32)
        hb = (jax.nn.silu(h1) * h3).astype(BF16)
        y_ref[...] = jnp.dot(hb, w2_ref[...].astype(BF16), preferred_element_type=F32)

    @pl.when(i >= nused_ref[0])
    def _():
        y_ref[...] = jnp.zeros_like(y_ref)


def _expert_ffn(blk_exp, nused, xs, w1, w3, w2, layer):
    n_rows, d = xs.shape
    n_blk = n_rows // MOE_BLOCK
    grid_spec = pltpu.PrefetchScalarGridSpec(
        num_scalar_prefetch=2,
        grid=(n_blk,),
        in_specs=[pl.BlockSpec((MOE_BLOCK, d), lambda i, be, nu: (i, 0)),
                  pl.BlockSpec((None, None, d, D_EXPERT), lambda i, be, nu: (layer, be[i], 0, 0)),
                  pl.BlockSpec((None, None, d, D_EXPERT), lambda i, be, nu: (layer, be[i], 0, 0)),
                  pl.BlockSpec((None, None, D_EXPERT, d), lambda i, be, nu: (layer, be[i], 0, 0))],
        out_specs=pl.BlockSpec((MOE_BLOCK, d), lambda i, be, nu: (i, 0)),
    )
    return pl.pallas_call(
        _expert_body,
        grid_spec=grid_spec,
        out_shape=jax.ShapeDtypeStruct((n_rows, d), F32),
        compiler_params=_cparams("arbitrary"),
        name="moe_experts",
    )(blk_exp, nused, xs, w1, w3, w2)


def _combine_body(dest_ref, dest_next_ref, route_ref, x1_ref, g_ref, b_ref, ys_ref, o_ref, buf_ref, sems):
    i = pl.program_id(0)
    n_steps = pl.num_programs(0)
    tm = x1_ref.shape[0]

    def row_copy(idx_ref, slot, k, r):
        return pltpu.make_async_copy(ys_ref.at[pl.ds(idx_ref[k, r], 1)], buf_ref.at[slot, k, pl.ds(r, 1)],
                                     sems.at[slot])

    def start_rows(idx_ref, slot):
        def issue(r, _):
            for k in range(TOP_K):
                row_copy(idx_ref, slot, k, r).start()
            return 0
        lax.fori_loop(0, tm, issue, 0)

    @pl.when(i == 0)
    def _():
        start_rows(dest_ref, 0)

    @pl.when(i + 1 < n_steps)
    def _():
        start_rows(dest_next_ref, (i + 1) % 2)

    slot = i % 2

    def drain(r, _):
        for k in range(TOP_K):
            row_copy(dest_ref, slot, k, r).wait()
        return 0

    lax.fori_loop(0, tm, drain, 0)
    ffn = jnp.zeros((tm, D_MODEL), F32)
    for k in range(TOP_K):
        gate = jnp.broadcast_to(route_ref[TOP_K + k:TOP_K + k + 1, :], (LANES, tm)).T
        ffn = ffn + jnp.tile(gate, (1, D_MODEL // LANES)) * buf_ref[slot, k]
    o_ref[...] = _layer_norm(ALPHA * x1_ref[...] + ffn, g_ref[...], b_ref[...])


def _combine_ln(dest, route, x1, ln_g, ln_b, ys):
    t, d = x1.shape
    full = lambda a: pl.BlockSpec(a.shape, lambda i: (0,) * a.ndim)
    n_steps = t // TM_ROWS
    return pl.pallas_call(
        _combine_body,
        grid=(n_steps,),
        in_specs=[pl.BlockSpec((TOP_K, TM_ROWS), lambda i: (0, i), memory_space=pltpu.SMEM),
                  pl.BlockSpec((TOP_K, TM_ROWS), lambda i: (0, jnp.minimum(i + 1, n_steps - 1)),
                               memory_space=pltpu.SMEM),
                  pl.BlockSpec((8, TM_ROWS), lambda i: (0, i)),
                  pl.BlockSpec((TM_ROWS, d), lambda i: (i, 0)), full(ln_g), full(ln_b),
                  pl.BlockSpec(memory_space=pl.ANY)],
        out_specs=pl.BlockSpec((TM_ROWS, d), lambda i: (i, 0)),
        out_shape=jax.ShapeDtypeStruct((t, d), F32),
        scratch_shapes=[pltpu.VMEM((2, TOP_K, TM_ROWS, d), F32), pltpu.SemaphoreType.DMA((2,))],
        compiler_params=_cparams("arbitrary"),
        name="moe_combine_ln",
    )(dest, dest, route, x1, ln_g, ln_b, ys)


def _moe_layer(x1, route, counts, w1, w3, w2, layer, ln_g, ln_b):
    t = x1.shape[0]
    n_rows = t * TOP_K + N_EXPERTS * MOE_BLOCK
    n_blk = n_rows // MOE_BLOCK
    cnt = counts[:, 0].astype(jnp.int32)
    padded = (cnt + MOE_BLOCK - 1) // MOE_BLOCK * MOE_BLOCK
    pend = jnp.cumsum(padded)
    pstart = (pend - padded).astype(F32).reshape(N_EXPERTS, 1)
    blk_row0 = jnp.arange(n_blk, dtype=jnp.int32) * MOE_BLOCK
    blk_exp = jnp.minimum(jnp.sum((pend[None, :] <= blk_row0[:, None]).astype(jnp.int32), axis=1), N_EXPERTS - 1)
    nused = (pend[-1:] // MOE_BLOCK).astype(jnp.int32)
    dest = _dest_rows(route, pstart)
    xs = _scatter_rows(dest, x1, n_rows)
    ys = _expert_ffn(blk_exp, nused, xs, w1, w3, w2, layer)
    return _combine_ln(dest, route, x1, ln_g, ln_b, ys)


def _rope_tables(seq):
    def tab(dim):
        inv = ROPE_THETA ** (-jnp.arange(0, dim, 2, dtype=F32) / dim)
        ang = jnp.arange(seq, dtype=F32)[:, None] * inv[None, :]
        return jnp.cos(ang), jnp.sin(ang)

    c, s = tab(HEAD_DIM)
    cos64 = jnp.tile(jnp.concatenate([c, c], axis=1), (1, LANES // HEAD_DIM))
    sin64 = jnp.tile(jnp.concatenate([-s, s], axis=1), (1, LANES // HEAD_DIM))
    c, s = tab(MLA_ROPE)
    pad_l, pad_r = MLA_ROPE_LANE, LANES - MLA_ROPE_LANE - MLA_ROPE
    cosm = jnp.concatenate([jnp.ones((seq, pad_l), F32), c, c, jnp.ones((seq, pad_r), F32)], axis=1)
    sinm = jnp.concatenate([jnp.zeros((seq, pad_l), F32), -s, s, jnp.zeros((seq, pad_r), F32)], axis=1)
    return cos64, sin64, cosm, sinm


def _prep_mixer_weights(w_in, q_norm, w_uq, kv_norm, w_ukv):
    d = w_in.shape[0]
    lat = MLA_Q_LORA + MLA_KV_LORA
    w_in_p = jnp.concatenate([
        w_in[:, :lat], jnp.zeros((d, MLA_ROPE_LANE), F32), w_in[:, lat:lat + MLA_ROPE],
        jnp.zeros((d, LANES - MLA_ROPE_LANE - MLA_ROPE), F32), w_in[:, lat + MLA_ROPE:]], axis=1).astype(BF16)
    wq = w_uq.reshape(MLA_Q_LORA, MLA_HEADS, MLA_NOPE + MLA_ROPE)
    wuq_p = jnp.pad(wq, ((0, 0), (0, 0), (0, MLA_HEAD_PAD - MLA_NOPE - MLA_ROPE)))
    wuq_p = wuq_p.reshape(MLA_Q_LORA, MLA_HEADS * MLA_HEAD_PAD).astype(BF16)
    wkv = w_ukv.reshape(MLA_KV_LORA, MLA_HEADS, MLA_NOPE + MLA_V)
    wk = jnp.pad(wkv[:, :, :MLA_NOPE], ((0, 0), (0, 0), (0, MLA_HEAD_PAD - MLA_NOPE)))
    wukv_p = jnp.concatenate([wk.reshape(MLA_KV_LORA, MLA_HEADS * MLA_HEAD_PAD),
                              wkv[:, :, MLA_NOPE:].reshape(MLA_KV_LORA, MLA_W)], axis=1).astype(BF16)
    return w_in_p, q_norm.reshape(1, -1), wuq_p, kv_norm.reshape(1, -1), wukv_p


def _mixer_heads(x2, w_in, q_norm, w_uq, kv_norm, w_ukv, tabs, batch, seq):
    w_in_p, qn, wuq_p, kvn, wukv_p = _prep_mixer_weights(w_in, q_norm, w_uq, kv_norm, w_ukv)
    qm, km, vm, qb, kb, vb, qc, kc, vc = _inproj(x2, w_in_p, qn, kvn, wuq_p, wukv_p, tabs, seq)
    o_a = _flash_attention(qm, km, vm, MLA_HEADS, batch, seq, "mla_attn")
    o_b = _flash_attention(qb, kb, vb, MOBA_HEADS, batch, seq, "moba_attn")
    o_c = _dil_attention(qc, kc, vc, batch, seq)
    return o_a, o_b, o_c


def kernel(x, w_in, mla_q_norm, mla_w_uq, mla_kv_norm, mla_w_ukv, w_out, ln1_g, ln1_b, router_w, router_b, moe_w1, moe_w3, moe_w2, ln2_g, ln2_b):
    batch, seq, d = x.shape
    tabs = _rope_tables(seq)
    x2 = x.reshape(batch * seq, d)
    rwt = router_w.T
    rb = router_b.reshape(N_EXPERTS, 1)
    for l in range(DEPTH):
        o_a, o_b, o_c = _mixer_heads(x2, w_in[l], mla_q_norm[l], mla_w_uq[l], mla_kv_norm[l], mla_w_ukv[l],
                                     tabs, batch, seq)
        x1, route, counts = _outproj_ln_router(o_a, o_b, o_c, w_out[l].astype(BF16), x2,
                                               ln1_g[l].reshape(1, d), ln1_b[l].reshape(1, d), rwt, rb)
        x2 = _moe_layer(x1, route, counts, moe_w1, moe_w3, moe_w2, l,
                        ln2_g[l].reshape(1, d), ln2_b[l].reshape(1, d))
    return x2.reshape(batch, seq, d)
```

```python
import functools

import jax
import jax.numpy as jnp
import numpy as np
from jax import lax
from jax.experimental import pallas as pl
from jax.experimental.pallas import tpu as pltpu

F32 = jnp.float32
BF16 = jnp.bfloat16
LANES = 128
NEG_INF = float("-inf")
VMEM_LIMIT_BYTES = 56 * 1024 * 1024

D_MODEL = 1024
DEPTH = 2
HEAD_DIM = 64
ROPE_THETA = 10000.0
MLA_HEADS = 4
MLA_NOPE = 64
MLA_ROPE = 32
MLA_V = 64
MLA_Q_LORA = 256
MLA_KV_LORA = 128
MOBA_HEADS = 6
MOBA_BLOCK = 256
MOBA_TOPK = 3
DIL_HEADS = 6
DIL_PAIRS = ((128, 1), (512, 4), (2048, 16))
MLA_W = MLA_HEADS * MLA_V
MOBA_W = MOBA_HEADS * HEAD_DIM
DIL_W = DIL_HEADS * HEAD_DIM
N_EXPERTS = 64
N_GROUPS = 8
EXP_PER_GROUP = N_EXPERTS // N_GROUPS
TOP_K = 2
D_EXPERT = 256
MOE_BLOCK = 256
ALPHA = (2 * DEPTH) ** 0.25
LN_EPS = 1e-5
RMS_EPS = 1e-6

SEG_A = 512
SEG_B = 3 * MOBA_W
SEG_C = 3 * DIL_W
N_IN_PAD = SEG_A + SEG_B + SEG_C
MLA_HEAD_PAD = 128
MLA_ROPE_LANE = 64

ROW_W = D_MODEL + 128
TM_PROJ = 512
TQ = 256
KB_WIDE = 4
DIL_W_UNITS = 128
LOG2_E = 1.4426950408889634
ONES_ROW = HEAD_DIM
SEL_LANE = HEAD_DIM
MASK_BIG = 2.0 ** 100


def _cparams(*sem):
    return pltpu.CompilerParams(dimension_semantics=sem, vmem_limit_bytes=VMEM_LIMIT_BYTES)


def _rope(x, cos, sin_signed, half):
    lane = lax.broadcasted_iota(jnp.int32, x.shape, 1)
    first = ((lane // half) % 2) == 0
    rot = jnp.where(first, pltpu.roll(x, LANES - half, 1), pltpu.roll(x, half, 1))
    return x * cos + rot * sin_signed


def _rms(x, g):
    return x * lax.rsqrt(jnp.mean(jnp.square(x), axis=-1, keepdims=True) + RMS_EPS) * g


def _inproj_body(seq_blocks, x_ref, w_ref, qn_ref, kvn_ref, wuq_ref, wukv_ref,
                 cos64_ref, sin64_ref, cosm_ref, sinm_ref,
                 qm_ref, km_ref, vm_ref, qb_ref, kb_ref, vb_ref, qc_ref, kc_ref, vc_ref,
                 kmean_ref):
    i = pl.program_id(0)
    tm = x_ref.shape[0]
    n_kb = kmean_ref.shape[0]
    sblk = i % seq_blocks

    @pl.when(sblk == 0)
    def _():
        kmean_ref[...] = jnp.zeros_like(kmean_ref)

    xb = x_ref[...].astype(BF16)
    cos64, sin64 = cos64_ref[...], sin64_ref[...]
    cosm, sinm = cosm_ref[...], sinm_ref[...]

    h_a = jnp.dot(xb, w_ref[:, 0:SEG_A], preferred_element_type=F32)
    cq = _rms(h_a[:, 0:MLA_Q_LORA], qn_ref[...]).astype(BF16)
    ckv = _rms(h_a[:, MLA_Q_LORA:MLA_Q_LORA + MLA_KV_LORA], kvn_ref[...]).astype(BF16)
    kr = _rope(h_a[:, MLA_Q_LORA + MLA_KV_LORA:SEG_A], cosm, sinm, MLA_ROPE // 2)
    q_m = jnp.dot(cq, wuq_ref[...], preferred_element_type=F32)
    kv_m = jnp.dot(ckv, wukv_ref[...], preferred_element_type=F32)
    mla_scale = (MLA_NOPE + MLA_ROPE) ** -0.5 * LOG2_E
    lane_t = lax.broadcasted_iota(jnp.int32, (tm, LANES), 1)
    row_t = lax.broadcasted_iota(jnp.int32, (LANES, tm), 0)
    zeros_half = jnp.zeros((LANES - HEAD_DIM, tm), F32)

    def put_t(ref, h, val_t):
        for b in range(tm // TQ):
            ref[b, h * LANES:(h + 1) * LANES, :] = val_t[:, b * TQ:(b + 1) * TQ].astype(ref.dtype)

    def v_heads_t(pair):
        pt = pair.T
        return [jnp.where(row_t == ONES_ROW, 1.0, jnp.concatenate([pt[s * HEAD_DIM:(s + 1) * HEAD_DIM], zeros_half], 0))
                for s in range(2)]

    for h in range(MLA_HEADS):
        sl = slice(h * MLA_HEAD_PAD, (h + 1) * MLA_HEAD_PAD)
        put_t(qm_ref, h, (_rope(q_m[:, sl], cosm, sinm, MLA_ROPE // 2) * mla_scale).T)
        km_ref[:, sl] = (kv_m[:, sl] + kr).astype(BF16)
    for c in range(MLA_HEADS // 2):
        v_pair = kv_m[:, (MLA_HEADS + c) * LANES:(MLA_HEADS + c + 1) * LANES]
        for s, v_t in enumerate(v_heads_t(v_pair)):
            put_t(vm_ref, 2 * c + s, v_t)

    h_b = jnp.dot(xb, w_ref[:, SEG_A:SEG_A + SEG_B], preferred_element_type=F32)
    scale = HEAD_DIM ** -0.5
    lane = lax.broadcasted_iota(jnp.int32, (n_kb, LANES), 1)
    pos = sblk * tm + lax.broadcasted_iota(jnp.int32, (1, tm), 1)
    q_blk = pos // MOBA_BLOCK
    jio = lax.broadcasted_iota(jnp.int32, (n_kb, tm), 0)
    elig = jio < q_blk
    row_blk = (sblk * tm + lax.broadcasted_iota(jnp.int32, (tm, LANES), 0)) // MOBA_BLOCK
    blk_lane = lane_t == SEL_LANE + row_blk
    first_head = lane_t < HEAD_DIM

    def split(pair):
        return (jnp.where(first_head, pair, 0.0), jnp.where(first_head, pltpu.roll(pair, HEAD_DIM, 1), 0.0))

    for c in range(MOBA_W // LANES):
        sl = slice(c * LANES, (c + 1) * LANES)
        q = _rope(h_b[:, sl], cos64, sin64, HEAD_DIM // 2)
        k = _rope(h_b[:, MOBA_W + c * LANES:MOBA_W + (c + 1) * LANES], cos64, sin64, HEAD_DIM // 2)
        km = kmean_ref[:, sl]
        kb_row = lax.broadcasted_iota(jnp.int32, (n_kb, LANES), 0)
        for b in range(tm // MOBA_BLOCK):
            mean_b = jnp.mean(k[b * MOBA_BLOCK:(b + 1) * MOBA_BLOCK], axis=0, keepdims=True)
            km = jnp.where(kb_row == sblk * (tm // MOBA_BLOCK) + b, mean_b, km)
        kmean_ref[:, sl] = km
        q_t = (q * (scale * LOG2_E)).T
        k_heads = split(k)
        v_t = v_heads_t(h_b[:, 2 * MOBA_W + c * LANES:2 * MOBA_W + (c + 1) * LANES])
        for half in range(2):
            hs = slice((2 * c + half) * LANES, (2 * c + half + 1) * LANES)
            in_head = (lane // HEAD_DIM) == half
            gate = lax.dot_general(jnp.where(in_head, km, 0.0), q, (((1,), (1,)), ((), ())),
                                   precision=lax.Precision.HIGHEST,
                                   preferred_element_type=F32)
            gate = jnp.where(elig, gate, NEG_INF)
            q_rows = [q_t[half * HEAD_DIM:(half + 1) * HEAD_DIM]]
            for j in range(n_kb):
                gj = gate[j:j + 1, :]
                beats = (gate > gj) | ((gate == gj) & (jio < j))
                cnt = jnp.sum(beats.astype(F32), axis=0, keepdims=True)
                keep = ((cnt < MOBA_TOPK) & elig[j:j + 1, :]) | (q_blk == j)
                q_rows.append(jnp.where(keep, 0.0, -MASK_BIG))
            q_rows.append(jnp.zeros((LANES - SEL_LANE - n_kb, tm), F32))
            put_t(qb_ref, 2 * c + half, jnp.concatenate(q_rows, axis=0))
            kb_ref[:, hs] = jnp.where(blk_lane, 1.0, k_heads[half]).astype(BF16)
            put_t(vb_ref, 2 * c + half, v_t[half])

    h_c = jnp.dot(xb, w_ref[:, SEG_A + SEG_B:], preferred_element_type=F32)
    for c in range(DIL_W // LANES):
        sl = slice(c * LANES, (c + 1) * LANES)
        qc_ref[:, sl] = _rope(h_c[:, sl], cos64, sin64, HEAD_DIM // 2) * (scale * LOG2_E)
        kc_ref[:, sl] = _rope(h_c[:, DIL_W + c * LANES:DIL_W + (c + 1) * LANES], cos64, sin64, HEAD_DIM // 2)
    vc_ref[...] = h_c[:, 2 * DIL_W:]


def _inproj(x2, w_in_p, qn, kvn, wuq_p, wukv_p, tabs, seq):
    t = x2.shape[0]
    tm = TM_PROJ
    seq_blocks = seq // tm
    n_kb = seq // MOBA_BLOCK
    row = lambda w: pl.BlockSpec((tm, w), lambda i: (i, 0))
    full = lambda a: pl.BlockSpec(a.shape, lambda i: (0,) * a.ndim)
    tab = pl.BlockSpec((tm, LANES), lambda i: (i % seq_blocks, 0))
    out_shapes = [
        jax.ShapeDtypeStruct((t // TQ, MLA_HEADS * LANES, TQ), BF16),
        jax.ShapeDtypeStruct((t, MLA_HEADS * MLA_HEAD_PAD), BF16),
        jax.ShapeDtypeStruct((t // TQ, MLA_HEADS * LANES, TQ), BF16),
        jax.ShapeDtypeStruct((t // TQ, MOBA_HEADS * LANES, TQ), BF16),
        jax.ShapeDtypeStruct((t, MOBA_HEADS * LANES), BF16),
        jax.ShapeDtypeStruct((t // TQ, MOBA_HEADS * LANES, TQ), BF16),
        jax.ShapeDtypeStruct((t, DIL_W), F32),
        jax.ShapeDtypeStruct((t, DIL_W), F32),
        jax.ShapeDtypeStruct((t, DIL_W), F32),
    ]
    return pl.pallas_call(
        functools.partial(_inproj_body, seq_blocks),
        grid=(t // tm,),
        in_specs=[row(D_MODEL), full(w_in_p), full(qn), full(kvn), full(wuq_p), full(wukv_p),
                  tab, tab, tab, tab],
        out_specs=[row(s.shape[1]) if len(s.shape) == 2 else
                   pl.BlockSpec((tm // TQ,) + s.shape[1:], lambda i: (i, 0, 0)) for s in out_shapes],
        out_shape=out_shapes,
        scratch_shapes=[pltpu.VMEM((n_kb, MOBA_W), F32)],
        compiler_params=_cparams("arbitrary"),
        name="inproj",
    )(x2, w_in_p, qn, kvn, wuq_p, wukv_p, *tabs)


def _flash_body(n_heads, qt_ref, k_ref, vt_ref, o_ref, *state):
    i = pl.program_id(1)
    tq = qt_ref.shape[1]

    def head(h):
        return slice(h * LANES, (h + 1) * LANES)

    m_refs, acc_refs = state[:n_heads], state[n_heads:]
    key_le_query = (lax.broadcasted_iota(jnp.int32, (tq, tq), 0) <= lax.broadcasted_iota(jnp.int32, (tq, tq), 1))

    def scores(j, n, h):
        rows = pl.ds(pl.multiple_of(j * tq, tq), n * tq)
        return jnp.dot(k_ref[rows, head(h)], qt_ref[head(h), :], preferred_element_type=F32)

    def pv(j, n, h, p):
        return sum(jnp.dot(vt_ref[j + b, head(h), :], p[b * tq:(b + 1) * tq], preferred_element_type=F32)
                   for b in range(n))

    def own_block(h, s):
        s = jnp.where(key_le_query, s, NEG_INF)
        m = jnp.max(s, axis=0, keepdims=True)
        acc_refs[h][...] = pv(i, 1, h, jnp.exp2(s - m).astype(BF16))
        m_refs[h][...] = m

    def past_blocks(j, n, h, s):
        m_old = m_refs[h][...]
        m_new = jnp.maximum(m_old, jnp.max(s, axis=0, keepdims=True))
        p = jnp.exp2(s - m_new).astype(BF16)
        acc_refs[h][...] = jnp.exp2(m_old - m_new) * acc_refs[h][...] + pv(j, n, h, p)
        m_refs[h][...] = m_new

    def sweep(j, n, update):
        s_next = scores(j, n, 0)
        for h in range(n_heads):
            s = s_next
            if h + 1 < n_heads:
                s_next = scores(j, n, h + 1)
            update(h, s)

    sweep(i, 1, own_block)

    def wide_body(jj, _):
        sweep(jj * KB_WIDE, KB_WIDE, functools.partial(past_blocks, jj * KB_WIDE, KB_WIDE))
        return 0

    def narrow_body(j, _):
        sweep(j, 1, functools.partial(past_blocks, j, 1))
        return 0

    n_wide = i // KB_WIDE
    lax.fori_loop(0, n_wide, wide_body, 0)
    lax.fori_loop(n_wide * KB_WIDE, i, narrow_body, 0)

    for c in range(n_heads // 2):
        halves = []
        for h in (2 * c, 2 * c + 1):
            acc = acc_refs[h][...]
            halves.append(acc[0:HEAD_DIM] / acc[ONES_ROW:ONES_ROW + 1])
        o_ref[:, head(c)] = jnp.concatenate(halves, axis=0).T.astype(o_ref.dtype)


def _flash_attention(qt, k, vt, n_heads, batch, seq, name):
    nq = seq // TQ
    width = n_heads * LANES
    return pl.pallas_call(
        functools.partial(_flash_body, n_heads),
        grid=(batch, nq),
        in_specs=[pl.BlockSpec((None, width, TQ), lambda b, i: (b * nq + i, 0, 0)),
                  pl.BlockSpec((seq, width), lambda b, i: (b, 0)),
                  pl.BlockSpec((nq, width, TQ), lambda b, i: (b, 0, 0))],
        out_specs=pl.BlockSpec((TQ, n_heads * HEAD_DIM), lambda b, i: (b * nq + i, 0)),
        out_shape=jax.ShapeDtypeStruct((batch * seq, n_heads * HEAD_DIM), BF16),
        scratch_shapes=[pltpu.VMEM((1, TQ), F32)] * n_heads + [pltpu.VMEM((LANES, TQ), F32)] * n_heads,
        compiler_params=_cparams("parallel", "arbitrary"),
        name=name,
    )(qt, k, vt)


def _dil_body(seq, q_ref, k_ref, v_ref, o_ref, ob_ref, lse_ref, ks_ref, vt_ref):
    w = DIL_W_UNITS
    n_all = seq // w
    frow = lax.broadcasted_iota(jnp.int32, (LANES, w), 0)
    key = lax.broadcasted_iota(jnp.int32, (2 * w, w), 0)
    qry = lax.broadcasted_iota(jnp.int32, (2 * w, w), 1)
    cur_ok = (key >= w) & (key - w <= qry)
    prev_ok = (key < w) & (key >= qry)
    zeros_half = jnp.zeros((LANES - HEAD_DIM, w), F32)
    ks_ref[0:w, :] = jnp.zeros((w, LANES), BF16)
    for h in range(2):
        vt_ref[h, 0] = jnp.zeros((LANES, w), BF16)
    for bi, (_, dil) in enumerate(DIL_PAIRS):
        n_blk = seq // (dil * w)

        def rows_of(idx, dil=dil, n_blk=n_blk):
            start = idx // n_blk + (idx % n_blk) * (w * dil)
            return pl.ds(start, w, stride=dil) if dil > 1 else pl.ds(pl.multiple_of(start, w), w)

        def stage(idx, _, rows_of=rows_of):
            rows = rows_of(idx)
            ks_ref[pl.ds(pl.multiple_of((idx + 1) * w, w), w), :] = k_ref[rows, :].astype(BF16)
            v_t = v_ref[rows, :].T
            for h in range(2):
                v_h = jnp.concatenate([v_t[h * HEAD_DIM:(h + 1) * HEAD_DIM], zeros_half], axis=0)
                vt_ref[h, idx + 1] = jnp.where(frow == ONES_ROW, 1.0, v_h).astype(BF16)
            return 0

        lax.fori_loop(0, n_all, stage, 0, unroll=2)

        def one(idx, _, bi=bi, n_blk=n_blk, rows_of=rows_of):
            rows = rows_of(idx)
            q_t = q_ref[rows, :].T
            keys = ks_ref[pl.ds(pl.multiple_of(idx * w, w), 2 * w), :]
            visible = cur_ok | (prev_ok & (idx % n_blk > 0))
            outs, lses = [], []
            for h in range(2):
                q_h = jnp.where((frow // HEAD_DIM) == h, q_t, 0.0).astype(BF16)
                s = jnp.where(visible, jnp.dot(keys, q_h, preferred_element_type=F32), NEG_INF)
                m = jnp.max(s, axis=0, keepdims=True)
                p = jnp.exp2(s - m).astype(BF16)
                acc = (jnp.dot(vt_ref[h, idx], p[0:w], preferred_element_type=F32)
                       + jnp.dot(vt_ref[h, idx + 1], p[w:2 * w], preferred_element_type=F32))
                l = acc[ONES_ROW:ONES_ROW + 1]
                outs.append(acc[0:HEAD_DIM] / l)
                lses.append(jnp.broadcast_to(m + jnp.log2(l), (HEAD_DIM, w)))
            ob_ref[bi, rows, :] = jnp.concatenate(outs, axis=0).T
            lse_ref[bi, rows, :] = jnp.concatenate(lses, axis=0).T
            return 0

        lax.fori_loop(0, n_all, one, 0, unroll=8)

    def merge(c, _):
        rows = pl.ds(pl.multiple_of(c * TQ, TQ), TQ)
        ls = [lse_ref[b, rows, :] for b in range(len(DIL_PAIRS))]
        top = functools.reduce(jnp.maximum, ls)
        ws = [jnp.exp2(x - top) for x in ls]
        num = sum(wb * ob_ref[b, rows, :] for b, wb in enumerate(ws))
        o_ref[rows, :] = (num / sum(ws)).astype(o_ref.dtype)
        return 0

    lax.fori_loop(0, seq // TQ, merge, 0)


def _dil_attention(qc, kc, vc, batch, seq):
    blk = pl.BlockSpec((seq, LANES), lambda b, g: (b, g))
    nbr = len(DIL_PAIRS)
    return pl.pallas_call(
        functools.partial(_dil_body, seq),
        grid=(batch, DIL_HEADS // 2),
        in_specs=[blk, blk, blk],
        out_specs=blk,
        out_shape=jax.ShapeDtypeStruct((batch * seq, DIL_W), BF16),
        scratch_shapes=[pltpu.VMEM((nbr, seq, LANES), F32), pltpu.VMEM((nbr, seq, LANES), F32),
                        pltpu.VMEM((seq + DIL_W_UNITS, LANES), BF16),
                        pltpu.VMEM((2, seq // DIL_W_UNITS + 1, LANES, DIL_W_UNITS), BF16)],
        compiler_params=_cparams("parallel", "parallel"),
        name="dil_attn",
    )(qc, kc, vc)


def _layer_norm(y, g, b):
    mu = jnp.mean(y, axis=-1, keepdims=True)
    yc = y - mu
    var = jnp.mean(jnp.square(yc), axis=-1, keepdims=True)
    return yc * lax.rsqrt(var + LN_EPS) * g + b


def _first_argmax(rows):
    best, idx = rows[0], jnp.zeros(rows[0].shape, jnp.int32)
    for j in range(1, len(rows)):
        upd = rows[j] > best
        idx = jnp.where(upd, j, idx)
        best = jnp.where(upd, rows[j], best)
    return best, idx


def _pick(rows, idx):
    out = rows[0]
    for j in range(1, len(rows)):
        out = jnp.where(idx == j, rows[j], out)
    return out


def _top2(rows):
    v1, i1 = _first_argmax(rows)
    rest = [jnp.where(i1 == j, NEG_INF, r) for j, r in enumerate(rows)]
    v2, i2 = _first_argmax(rest)
    return v1, i1, v2, i2


def _outproj_body(oa_ref, ob_ref, oc_ref, wo_ref, x_ref, g_ref, b_ref, rwt_ref, rb_ref,
                  x1_ref, grp_ref, cnt_ref):
    i = pl.program_id(0)
    tm = x_ref.shape[0]

    @pl.when(i == 0)
    def _():
        cnt_ref[...] = jnp.zeros_like(cnt_ref)

    mix = (jnp.dot(oa_ref[...], wo_ref[0:MLA_W, :], preferred_element_type=F32)
           + jnp.dot(ob_ref[...], wo_ref[MLA_W:MLA_W + MOBA_W, :], preferred_element_type=F32)
           + jnp.dot(oc_ref[...], wo_ref[MLA_W + MOBA_W:, :], preferred_element_type=F32))
    x1 = _layer_norm(ALPHA * x_ref[...] + mix, g_ref[...], b_ref[...])
    x1_ref[:, 0:D_MODEL] = x1

    logits = lax.dot_general(rwt_ref[...], x1, (((1,), (1,)), ((), ())),
                             precision=lax.Precision.HIGHEST, preferred_element_type=F32)
    s = jax.nn.sigmoid(logits)
    sb = s + rb_ref[...]
    s_rows = [s[e:e + 1, :] for e in range(N_EXPERTS)]
    sb_rows = [sb[e:e + 1, :] for e in range(N_EXPERTS)]
    grp = lambda rows, g: rows[g * EXP_PER_GROUP:(g + 1) * EXP_PER_GROUP]
    g_scores = []
    for g in range(N_GROUPS):
        v1, _, v2, _ = _top2(grp(sb_rows, g))
        g_scores.append(v1 + v2)
    _, g_sel = _first_argmax(g_scores)
    in_b = [_pick([grp(sb_rows, g)[j] for g in range(N_GROUPS)], g_sel) for j in range(EXP_PER_GROUP)]
    in_s = [_pick([grp(s_rows, g)[j] for g in range(N_GROUPS)], g_sel) for j in range(EXP_PER_GROUP)]
    _, j1, _, j2 = _top2(in_b)
    s1, s2 = _pick(in_s, j1), _pick(in_s, j2)
    den = s1 + s2
    g1, g2 = s1 / den, s2 / den
    gate_rows = [jnp.where(j1 == j, g1, 0.0) + jnp.where(j2 == j, g2, 0.0) for j in range(EXP_PER_GROUP)]
    gate_rows.append(jnp.zeros((LANES - EXP_PER_GROUP, tm), F32))
    x1_ref[:, D_MODEL:] = jnp.concatenate(gate_rows, axis=0).T
    grp_ref[...] = jnp.concatenate([g_sel.astype(F32), jnp.zeros((7, tm), F32)], axis=0)
    gio = lax.broadcasted_iota(jnp.int32, (N_GROUPS, tm), 0)
    cnt_ref[...] += jnp.sum(jnp.where(gio == g_sel, 1.0, 0.0), axis=1, keepdims=True)


def _outproj_ln_router(o_a, o_b, o_c, w_out_b, x2, ln_g, ln_b, rwt, rb):
    t = x2.shape[0]
    tm = TM_PROJ
    row = lambda w: pl.BlockSpec((tm, w), lambda i: (i, 0))
    full = lambda a: pl.BlockSpec(a.shape, lambda i: (0,) * a.ndim)
    return pl.pallas_call(
        _outproj_body,
        grid=(t // tm,),
        in_specs=[row(MLA_W), row(MOBA_W), row(DIL_W), full(w_out_b), row(D_MODEL), full(ln_g), full(ln_b),
                  full(rwt), full(rb)],
        out_specs=[row(ROW_W), pl.BlockSpec((8, tm), lambda i: (0, i)),
                   pl.BlockSpec((N_GROUPS, LANES), lambda i: (0, 0))],
        out_shape=[jax.ShapeDtypeStruct((t, ROW_W), F32), jax.ShapeDtypeStruct((8, t), F32),
                   jax.ShapeDtypeStruct((N_GROUPS, LANES), F32)],
        compiler_params=_cparams("arbitrary"),
        name="outproj_ln_router",
    )(o_a, o_b, o_c, w_out_b, x2, ln_g, ln_b, rwt, rb)


TN_SORT = 512
TM_ROWS = 256


def _dest_body(grp_ref, pstart_ref, dest_ref, carry_ref):
    n = grp_ref.shape[1]

    @pl.when(pl.program_id(0) == 0)
    def _():
        carry_ref[...] = jnp.broadcast_to(pstart_ref[...], carry_ref.shape)

    before = (lax.broadcasted_iota(jnp.int32, (n, n), 0) < lax.broadcasted_iota(jnp.int32, (n, n), 1))
    before = jnp.where(before, 1.0, 0.0).astype(BF16)
    gio = lax.broadcasted_iota(jnp.int32, (N_GROUPS, n), 0)
    hit = jnp.where(gio == grp_ref[0:1, :].astype(jnp.int32), 1.0, 0.0)
    earlier = jnp.dot(hit.astype(BF16), before, preferred_element_type=F32)
    dest = jnp.sum(hit * (earlier + carry_ref[:, 0:1]), axis=0, keepdims=True)
    carry_ref[...] += jnp.sum(hit, axis=1, keepdims=True)
    dest_ref[...] = dest.astype(jnp.int32)


def _dest_rows(grp, pstart):
    t = grp.shape[1]
    return pl.pallas_call(
        _dest_body,
        grid=(t // TN_SORT,),
        in_specs=[pl.BlockSpec((8, TN_SORT), lambda i: (0, i)),
                  pl.BlockSpec((N_GROUPS, 1), lambda i: (0, 0))],
        out_specs=pl.BlockSpec((1, TN_SORT), lambda i: (0, i)),
        out_shape=jax.ShapeDtypeStruct((1, t), jnp.int32),
        scratch_shapes=[pltpu.VMEM((N_GROUPS, LANES), F32)],
        compiler_params=_cparams("arbitrary"),
        name="moe_dest",
    )(grp, pstart)


def _scatter_body(dest_ref, x_ref, xs_in_ref, xs_ref, sem):
    del xs_in_ref
    tm = x_ref.shape[0]

    def row_copy(r):
        return pltpu.make_async_copy(x_ref.at[pl.ds(r, 1)], xs_ref.at[pl.ds(dest_ref[0, r], 1)], sem)

    def issue(r, _):
        row_copy(r).start()
        return 0

    def drain(r, _):
        row_copy(r).wait()
        return 0

    lax.fori_loop(0, tm, issue, 0)
    lax.fori_loop(0, tm, drain, 0)


def _scatter_rows(dest, x1, n_rows):
    t, d = x1.shape
    xs_init = jnp.zeros((n_rows, d), x1.dtype)
    return pl.pallas_call(
        _scatter_body,
        grid=(t // TM_ROWS,),
        in_specs=[pl.BlockSpec((1, TM_ROWS), lambda i: (0, i), memory_space=pltpu.SMEM),
                  pl.BlockSpec((TM_ROWS, d), lambda i: (i, 0)),
                  pl.BlockSpec(memory_space=pl.ANY)],
        out_specs=pl.BlockSpec(memory_space=pl.ANY),
        out_shape=jax.ShapeDtypeStruct((n_rows, d), x1.dtype),
        scratch_shapes=[pltpu.SemaphoreType.DMA(())],
        input_output_aliases={2: 0},
        compiler_params=_cparams("arbitrary"),
        name="moe_scatter",
    )(dest, x1, xs_init)


def _expert_body(blk_grp_ref, nused_ref, xs_ref, w1_ref, w3_ref, w2_ref, y_ref):
    del blk_grp_ref
    i = pl.program_id(0)

    @pl.when(i < nused_ref[0])
    def _():
        xb = xs_ref[:, 0:D_MODEL].astype(BF16)
        gates = xs_ref[:, D_MODEL:]
        y = jnp.zeros(y_ref.shape, F32)
        for j in range(EXP_PER_GROUP):
            h1 = jnp.dot(xb, w1_ref[j], preferred_element_type=F32)
            h3 = jnp.dot(xb, w3_ref[j], preferred_element_type=F32)
            hb = (jax.nn.silu(h1) * h3).astype(BF16)
            y = y + jnp.dot(hb, w2_ref[j], preferred_element_type=F32) * gates[:, j:j + 1]
        y_ref[...] = y

    @pl.when(i >= nused_ref[0])
    def _():
        y_ref[...] = jnp.zeros_like(y_ref)


def _expert_ffn(blk_grp, nused, xs, w1, w3, w2, layer):
    n_rows = xs.shape[0]
    d = D_MODEL
    n_blk = n_rows // MOE_BLOCK
    grp_w = lambda r, c: pl.BlockSpec((None, None, EXP_PER_GROUP, r, c),
                                      lambda i, bg, nu: (layer, bg[i], 0, 0, 0))
    grid_spec = pltpu.PrefetchScalarGridSpec(
        num_scalar_prefetch=2,
        grid=(n_blk,),
        in_specs=[pl.BlockSpec((MOE_BLOCK, ROW_W), lambda i, bg, nu: (i, 0)),
                  grp_w(d, D_EXPERT), grp_w(d, D_EXPERT), grp_w(D_EXPERT, d)],
        out_specs=pl.BlockSpec((MOE_BLOCK, d), lambda i, bg, nu: (i, 0)),
    )
    return pl.pallas_call(
        _expert_body,
        grid_spec=grid_spec,
        out_shape=jax.ShapeDtypeStruct((n_rows, d), F32),
        compiler_params=_cparams("arbitrary"),
        name="moe_experts",
    )(blk_grp, nused, xs, w1, w3, w2)


def _combine_body(dest_ref, dest_next_ref, x1_ref, g_ref, b_ref, ys_ref, o_ref, buf_ref, sems):
    i = pl.program_id(0)
    n_steps = pl.num_programs(0)
    tm = x1_ref.shape[0]

    def row_copy(idx_ref, slot, r):
        return pltpu.make_async_copy(ys_ref.at[pl.ds(idx_ref[0, r], 1)], buf_ref.at[slot, pl.ds(r, 1)],
                                     sems.at[slot])

    def start_rows(idx_ref, slot):
        def issue(r, _):
            row_copy(idx_ref, slot, r).start()
            return 0
        lax.fori_loop(0, tm, issue, 0)

    @pl.when(i == 0)
    def _():
        start_rows(dest_ref, 0)

    @pl.when(i + 1 < n_steps)
    def _():
        start_rows(dest_next_ref, (i + 1) % 2)

    slot = i % 2

    def drain(r, _):
        row_copy(dest_ref, slot, r).wait()
        return 0

    lax.fori_loop(0, tm, drain, 0)
    o_ref[...] = _layer_norm(ALPHA * x1_ref[...] + buf_ref[slot], g_ref[...], b_ref[...])


def _combine_ln(dest, x1, ln_g, ln_b, ys):
    t = x1.shape[0]
    d = D_MODEL
    full = lambda a: pl.BlockSpec(a.shape, lambda i: (0,) * a.ndim)
    n_steps = t // TM_ROWS
    return pl.pallas_call(
        _combine_body,
        grid=(n_steps,),
        in_specs=[pl.BlockSpec((1, TM_ROWS), lambda i: (0, i), memory_space=pltpu.SMEM),
                  pl.BlockSpec((1, TM_ROWS), lambda i: (0, jnp.minimum(i + 1, n_steps - 1)),
                               memory_space=pltpu.SMEM),
                  pl.BlockSpec((TM_ROWS, d), lambda i: (i, 0)), full(ln_g), full(ln_b),
                  pl.BlockSpec(memory_space=pl.ANY)],
        out_specs=pl.BlockSpec((TM_ROWS, d), lambda i: (i, 0)),
        out_shape=jax.ShapeDtypeStruct((t, d), F32),
        scratch_shapes=[pltpu.VMEM((2, TM_ROWS, d), F32), pltpu.SemaphoreType.DMA((2,))],
        compiler_params=_cparams("arbitrary"),
        name="moe_combine_ln",
    )(dest, dest, x1, ln_g, ln_b, ys)


def _moe_layer(x1, grp, counts, w1, w3, w2, layer, ln_g, ln_b):
    t = x1.shape[0]
    n_rows = t + N_GROUPS * MOE_BLOCK
    n_blk = n_rows // MOE_BLOCK
    cnt = counts[:, 0].astype(jnp.int32)
    padded = (cnt + MOE_BLOCK - 1) // MOE_BLOCK * MOE_BLOCK
    pend = jnp.cumsum(padded)
    pstart = (pend - padded).astype(F32).reshape(N_GROUPS, 1)
    blk_row0 = jnp.arange(n_blk, dtype=jnp.int32) * MOE_BLOCK
    blk_grp = jnp.minimum(jnp.sum((pend[None, :] <= blk_row0[:, None]).astype(jnp.int32), axis=1), N_GROUPS - 1)
    nused = (pend[-1:] // MOE_BLOCK).astype(jnp.int32)
    dest = _dest_rows(grp, pstart)
    xs = _scatter_rows(dest, x1, n_rows)
    ys = _expert_ffn(blk_grp, nused, xs, w1, w3, w2, layer)
    return _combine_ln(dest, x1, ln_g, ln_b, ys)


def _rope_tables(seq):
    def tab(dim):
        inv = ROPE_THETA ** (-jnp.arange(0, dim, 2, dtype=F32) / dim)
        ang = jnp.arange(seq, dtype=F32)[:, None] * inv[None, :]
        return jnp.cos(ang), jnp.sin(ang)

    c, s = tab(HEAD_DIM)
    cos64 = jnp.tile(jnp.concatenate([c, c], axis=1), (1, LANES // HEAD_DIM))
    sin64 = jnp.tile(jnp.concatenate([-s, s], axis=1), (1, LANES // HEAD_DIM))
    c, s = tab(MLA_ROPE)
    pad_l, pad_r = MLA_ROPE_LANE, LANES - MLA_ROPE_LANE - MLA_ROPE
    cosm = jnp.concatenate([jnp.ones((seq, pad_l), F32), c, c, jnp.ones((seq, pad_r), F32)], axis=1)
    sinm = jnp.concatenate([jnp.zeros((seq, pad_l), F32), -s, s, jnp.zeros((seq, pad_r), F32)], axis=1)
    return cos64, sin64, cosm, sinm


def _prep_mixer_weights(w_in, q_norm, w_uq, kv_norm, w_ukv):
    d = w_in.shape[0]
    lat = MLA_Q_LORA + MLA_KV_LORA
    w_in_p = jnp.concatenate([
        w_in[:, :lat], jnp.zeros((d, MLA_ROPE_LANE), F32), w_in[:, lat:lat + MLA_ROPE],
        jnp.zeros((d, LANES - MLA_ROPE_LANE - MLA_ROPE), F32), w_in[:, lat + MLA_ROPE:]], axis=1).astype(BF16)
    wq = w_uq.reshape(MLA_Q_LORA, MLA_HEADS, MLA_NOPE + MLA_ROPE)
    wuq_p = jnp.pad(wq, ((0, 0), (0, 0), (0, MLA_HEAD_PAD - MLA_NOPE - MLA_ROPE)))
    wuq_p = wuq_p.reshape(MLA_Q_LORA, MLA_HEADS * MLA_HEAD_PAD).astype(BF16)
    wkv = w_ukv.reshape(MLA_KV_LORA, MLA_HEADS, MLA_NOPE + MLA_V)
    wk = jnp.pad(wkv[:, :, :MLA_NOPE], ((0, 0), (0, 0), (0, MLA_HEAD_PAD - MLA_NOPE)))
    wukv_p = jnp.concatenate([wk.reshape(MLA_KV_LORA, MLA_HEADS * MLA_HEAD_PAD),
                              wkv[:, :, MLA_NOPE:].reshape(MLA_KV_LORA, MLA_W)], axis=1).astype(BF16)
    return w_in_p, q_norm.reshape(1, -1), wuq_p, kv_norm.reshape(1, -1), wukv_p


def _mixer_heads(x2, w_in, q_norm, w_uq, kv_norm, w_ukv, tabs, batch, seq):
    w_in_p, qn, wuq_p, kvn, wukv_p = _prep_mixer_weights(w_in, q_norm, w_uq, kv_norm, w_ukv)
    qm, km, vm, qb, kb, vb, qc, kc, vc = _inproj(x2, w_in_p, qn, kvn, wuq_p, wukv_p, tabs, seq)
    o_a = _flash_attention(qm, km, vm, MLA_HEADS, batch, seq, "mla_attn")
    o_b = _flash_attention(qb, kb, vb, MOBA_HEADS, batch, seq, "moba_attn")
    o_c = _dil_attention(qc, kc, vc, batch, seq)
    return o_a, o_b, o_c


def kernel(x, w_in, mla_q_norm, mla_w_uq, mla_kv_norm, mla_w_ukv, w_out, ln1_g, ln1_b, router_w, router_b, moe_w1, moe_w3, moe_w2, ln2_g, ln2_b):
    batch, seq, d = x.shape
    tabs = _rope_tables(seq)
    x2 = x.reshape(batch * seq, d)
    rwt = router_w.T
    rb = router_b.reshape(N_EXPERTS, 1)
    by_group = lambda w: w.astype(BF16).reshape(DEPTH, N_GROUPS, EXP_PER_GROUP, *w.shape[2:])
    w1g, w3g, w2g = by_group(moe_w1), by_group(moe_w3), by_group(moe_w2)
    for l in range(DEPTH):
        o_a, o_b, o_c = _mixer_heads(x2, w_in[l], mla_q_norm[l], mla_w_uq[l], mla_kv_norm[l], mla_w_ukv[l],
                                     tabs, batch, seq)
        x1, route, counts = _outproj_ln_router(o_a, o_b, o_c, w_out[l].astype(BF16), x2,
                                               ln1_g[l].reshape(1, d), ln1_b[l].reshape(1, d), rwt, rb)
        x2 = _moe_layer(x1, route, counts, w1g, w3g, w2g, l,
                        ln2_g[l].reshape(1, d), ln2_b[l].reshape(1, d))
    return x2.reshape(batch, seq, d)
```

```python
import functools

import jax
import jax.numpy as jnp
import numpy as np
from jax import lax
from jax.experimental import pallas as pl
from jax.experimental.pallas import tpu as pltpu

F32 = jnp.float32
BF16 = jnp.bfloat16
LANES = 128
NEG_INF = float("-inf")
VMEM_LIMIT_BYTES = 56 * 1024 * 1024

D_MODEL = 1024
DEPTH = 2
HEAD_DIM = 64
ROPE_THETA = 10000.0
MLA_HEADS = 4
MLA_NOPE = 64
MLA_ROPE = 32
MLA_V = 64
MLA_Q_LORA = 256
MLA_KV_LORA = 128
MOBA_HEADS = 6
MOBA_BLOCK = 256
MOBA_TOPK = 3
DIL_HEADS = 6
DIL_PAIRS = ((128, 1), (512, 4), (2048, 16))
MLA_W = MLA_HEADS * MLA_V
MOBA_W = MOBA_HEADS * HEAD_DIM
DIL_W = DIL_HEADS * HEAD_DIM
N_EXPERTS = 64
N_GROUPS = 8
EXP_PER_GROUP = N_EXPERTS // N_GROUPS
TOP_K = 2
D_EXPERT = 256
MOE_BLOCK = 512
ALPHA = (2 * DEPTH) ** 0.25
LN_EPS = 1e-5
RMS_EPS = 1e-6

SEG_A = 512
SEG_B = 3 * MOBA_W
SEG_C = 3 * DIL_W
N_IN_PAD = SEG_A + SEG_B + SEG_C
MLA_HEAD_PAD = 128
MLA_ROPE_LANE = 64

ROW_W = D_MODEL + 128
TM_PROJ = 512
TQ = 256
KB_WIDE = 4
DIL_W_UNITS = 128
LOG2_E = 1.4426950408889634
ONES_ROW = HEAD_DIM
SEL_LANE = HEAD_DIM
MASK_BIG = 2.0 ** 100


def _cparams(*sem):
    return pltpu.CompilerParams(dimension_semantics=sem, vmem_limit_bytes=VMEM_LIMIT_BYTES)


def _rope(x, cos, sin_signed, half):
    lane = lax.broadcasted_iota(jnp.int32, x.shape, 1)
    first = ((lane // half) % 2) == 0
    rot = jnp.where(first, pltpu.roll(x, LANES - half, 1), pltpu.roll(x, half, 1))
    return x * cos + rot * sin_signed


def _split_bf16(x):
    hi = x.astype(BF16)
    return hi, (x - hi.astype(F32)).astype(BF16)


def _dot_nt_split(a_parts, b_parts):
    dn = (((1,), (1,)), ((), ()))
    dot = lambda x, y: lax.dot_general(x, y, dn, preferred_element_type=F32)
    (a_hi, a_lo), (b_hi, b_lo) = a_parts, b_parts
    return dot(a_hi, b_hi) + (dot(a_hi, b_lo) + dot(a_lo, b_hi))


def _rms(x, g):
    return x * lax.rsqrt(jnp.mean(jnp.square(x), axis=-1, keepdims=True) + RMS_EPS) * g


def _inproj_body(seq_blocks, x_ref, w_ref, qn_ref, kvn_ref, wuq_ref, wukv_ref,
                 cos64_ref, sin64_ref, cosm_ref, sinm_ref,
                 qm_ref, km_ref, vm_ref, qb_ref, kb_ref, vb_ref, qc_ref, kc_ref, vc_ref,
                 kmean_ref):
    i = pl.program_id(0)
    tm = x_ref.shape[0]
    n_kb = kmean_ref.shape[0]
    sblk = i % seq_blocks

    @pl.when(sblk == 0)
    def _():
        kmean_ref[...] = jnp.zeros_like(kmean_ref)

    xb = x_ref[...].astype(BF16)
    cos64, sin64 = cos64_ref[...], sin64_ref[...]
    cosm, sinm = cosm_ref[...], sinm_ref[...]

    h_a = jnp.dot(xb, w_ref[:, 0:SEG_A], preferred_element_type=F32)
    cq = _rms(h_a[:, 0:MLA_Q_LORA], qn_ref[...]).astype(BF16)
    ckv = _rms(h_a[:, MLA_Q_LORA:MLA_Q_LORA + MLA_KV_LORA], kvn_ref[...]).astype(BF16)
    kr = _rope(h_a[:, MLA_Q_LORA + MLA_KV_LORA:SEG_A], cosm, sinm, MLA_ROPE // 2)
    q_m = jnp.dot(cq, wuq_ref[...], preferred_element_type=F32)
    kv_m = jnp.dot(ckv, wukv_ref[...], preferred_element_type=F32)
    mla_scale = (MLA_NOPE + MLA_ROPE) ** -0.5 * LOG2_E
    lane_t = lax.broadcasted_iota(jnp.int32, (tm, LANES), 1)
    row_t = lax.broadcasted_iota(jnp.int32, (LANES, tm), 0)
    zeros_half = jnp.zeros((LANES - HEAD_DIM, tm), F32)

    def put_t(ref, h, val_t):
        for b in range(tm // TQ):
            ref[b, h * LANES:(h + 1) * LANES, :] = val_t[:, b * TQ:(b + 1) * TQ].astype(ref.dtype)

    def v_heads_t(pair):
        pt = pair.T
        return [jnp.where(row_t == ONES_ROW, 1.0, jnp.concatenate([pt[s * HEAD_DIM:(s + 1) * HEAD_DIM], zeros_half], 0))
                for s in range(2)]

    for h in range(MLA_HEADS):
        sl = slice(h * MLA_HEAD_PAD, (h + 1) * MLA_HEAD_PAD)
        put_t(qm_ref, h, (_rope(q_m[:, sl], cosm, sinm, MLA_ROPE // 2) * mla_scale).T)
        km_ref[:, sl] = (kv_m[:, sl] + kr).astype(BF16)
    for c in range(MLA_HEADS // 2):
        v_pair = kv_m[:, (MLA_HEADS + c) * LANES:(MLA_HEADS + c + 1) * LANES]
        for s, v_t in enumerate(v_heads_t(v_pair)):
            put_t(vm_ref, 2 * c + s, v_t)

    h_b = jnp.dot(xb, w_ref[:, SEG_A:SEG_A + SEG_B], preferred_element_type=F32)
    scale = HEAD_DIM ** -0.5
    lane = lax.broadcasted_iota(jnp.int32, (n_kb, LANES), 1)
    pos = sblk * tm + lax.broadcasted_iota(jnp.int32, (1, tm), 1)
    q_blk = pos // MOBA_BLOCK
    jio = lax.broadcasted_iota(jnp.int32, (n_kb, tm), 0)
    elig = jio < q_blk
    row_blk = (sblk * tm + lax.broadcasted_iota(jnp.int32, (tm, LANES), 0)) // MOBA_BLOCK
    blk_lane = lane_t == SEL_LANE + row_blk
    first_head = lane_t < HEAD_DIM

    def split(pair):
        return (jnp.where(first_head, pair, 0.0), jnp.where(first_head, pltpu.roll(pair, HEAD_DIM, 1), 0.0))

    for c in range(MOBA_W // LANES):
        sl = slice(c * LANES, (c + 1) * LANES)
        q = _rope(h_b[:, sl], cos64, sin64, HEAD_DIM // 2)
        k = _rope(h_b[:, MOBA_W + c * LANES:MOBA_W + (c + 1) * LANES], cos64, sin64, HEAD_DIM // 2)
        km = kmean_ref[:, sl]
        kb_row = lax.broadcasted_iota(jnp.int32, (n_kb, LANES), 0)
        for b in range(tm // MOBA_BLOCK):
            mean_b = jnp.mean(k[b * MOBA_BLOCK:(b + 1) * MOBA_BLOCK], axis=0, keepdims=True)
            km = jnp.where(kb_row == sblk * (tm // MOBA_BLOCK) + b, mean_b, km)
        kmean_ref[:, sl] = km
        q_t = (q * (scale * LOG2_E)).T
        k_heads = split(k)
        q_parts = _split_bf16(q)
        v_t = v_heads_t(h_b[:, 2 * MOBA_W + c * LANES:2 * MOBA_W + (c + 1) * LANES])
        for half in range(2):
            hs = slice((2 * c + half) * LANES, (2 * c + half + 1) * LANES)
            in_head = (lane // HEAD_DIM) == half
            gate = _dot_nt_split(_split_bf16(jnp.where(in_head, km, 0.0)), q_parts)
            gate = jnp.where(elig, gate, NEG_INF)
            q_rows = [q_t[half * HEAD_DIM:(half + 1) * HEAD_DIM]]
            for j in range(n_kb):
                gj = gate[j:j + 1, :]
                beats = (gate > gj) | ((gate == gj) & (jio < j))
                cnt = jnp.sum(beats.astype(F32), axis=0, keepdims=True)
                keep = ((cnt < MOBA_TOPK) & elig[j:j + 1, :]) | (q_blk == j)
                q_rows.append(jnp.where(keep, 0.0, -MASK_BIG))
            q_rows.append(jnp.zeros((LANES - SEL_LANE - n_kb, tm), F32))
            put_t(qb_ref, 2 * c + half, jnp.concatenate(q_rows, axis=0))
            kb_ref[:, hs] = jnp.where(blk_lane, 1.0, k_heads[half]).astype(BF16)
            put_t(vb_ref, 2 * c + half, v_t[half])

    h_c = jnp.dot(xb, w_ref[:, SEG_A + SEG_B:], preferred_element_type=F32)
    for c in range(DIL_W // LANES):
        sl = slice(c * LANES, (c + 1) * LANES)
        qc_ref[:, sl] = _rope(h_c[:, sl], cos64, sin64, HEAD_DIM // 2) * (scale * LOG2_E)
        kc_ref[:, sl] = _rope(h_c[:, DIL_W + c * LANES:DIL_W + (c + 1) * LANES], cos64, sin64, HEAD_DIM // 2)
    vc_ref[...] = h_c[:, 2 * DIL_W:]


def _inproj(x2, w_in_p, qn, kvn, wuq_p, wukv_p, tabs, seq):
    t = x2.shape[0]
    tm = TM_PROJ
    seq_blocks = seq // tm
    n_kb = seq // MOBA_BLOCK
    row = lambda w: pl.BlockSpec((tm, w), lambda i: (i, 0))
    full = lambda a: pl.BlockSpec(a.shape, lambda i: (0,) * a.ndim)
    tab = pl.BlockSpec((tm, LANES), lambda i: (i % seq_blocks, 0))
    out_shapes = [
        jax.ShapeDtypeStruct((t // TQ, MLA_HEADS * LANES, TQ), BF16),
        jax.ShapeDtypeStruct((t, MLA_HEADS * MLA_HEAD_PAD), BF16),
        jax.ShapeDtypeStruct((t // TQ, MLA_HEADS * LANES, TQ), BF16),
        jax.ShapeDtypeStruct((t // TQ, MOBA_HEADS * LANES, TQ), BF16),
        jax.ShapeDtypeStruct((t, MOBA_HEADS * LANES), BF16),
        jax.ShapeDtypeStruct((t // TQ, MOBA_HEADS * LANES, TQ), BF16),
        jax.ShapeDtypeStruct((t, DIL_W), F32),
        jax.ShapeDtypeStruct((t, DIL_W), F32),
        jax.ShapeDtypeStruct((t, DIL_W), F32),
    ]
    return pl.pallas_call(
        functools.partial(_inproj_body, seq_blocks),
        grid=(t // tm,),
        in_specs=[row(D_MODEL), full(w_in_p), full(qn), full(kvn), full(wuq_p), full(wukv_p),
                  tab, tab, tab, tab],
        out_specs=[row(s.shape[1]) if len(s.shape) == 2 else
                   pl.BlockSpec((tm // TQ,) + s.shape[1:], lambda i: (i, 0, 0)) for s in out_shapes],
        out_shape=out_shapes,
        scratch_shapes=[pltpu.VMEM((n_kb, MOBA_W), F32)],
        compiler_params=_cparams("arbitrary"),
        name="inproj",
    )(x2, w_in_p, qn, kvn, wuq_p, wukv_p, *tabs)


def _flash_body(n_heads, qt_ref, k_ref, vt_ref, o_ref, *state):
    i = pl.program_id(1)
    tq = qt_ref.shape[1]

    def head(h):
        return slice(h * LANES, (h + 1) * LANES)

    m_refs, acc_refs = state[:n_heads], state[n_heads:]
    key_le_query = (lax.broadcasted_iota(jnp.int32, (tq, tq), 0) <= lax.broadcasted_iota(jnp.int32, (tq, tq), 1))

    def scores(j, n, h):
        rows = pl.ds(pl.multiple_of(j * tq, tq), n * tq)
        return jnp.dot(k_ref[rows, head(h)], qt_ref[head(h), :], preferred_element_type=F32)

    def pv(j, n, h, p):
        return sum(jnp.dot(vt_ref[j + b, head(h), :], p[b * tq:(b + 1) * tq], preferred_element_type=F32)
                   for b in range(n))

    def own_block(h, s):
        s = jnp.where(key_le_query, s, NEG_INF)
        m = jnp.max(s, axis=0, keepdims=True)
        acc_refs[h][...] = pv(i, 1, h, jnp.exp2(s - m).astype(BF16))
        m_refs[h][...] = m

    def past_blocks(j, n, h, s):
        m_old = m_refs[h][...]
        m_new = jnp.maximum(m_old, jnp.max(s, axis=0, keepdims=True))
        p = jnp.exp2(s - m_new).astype(BF16)
        acc_refs[h][...] = jnp.exp2(m_old - m_new) * acc_refs[h][...] + pv(j, n, h, p)
        m_refs[h][...] = m_new

    def sweep(j, n, update):
        s_next = scores(j, n, 0)
        for h in range(n_heads):
            s = s_next
            if h + 1 < n_heads:
                s_next = scores(j, n, h + 1)
            update(h, s)

    sweep(i, 1, own_block)

    def wide_body(jj, _):
        sweep(jj * KB_WIDE, KB_WIDE, functools.partial(past_blocks, jj * KB_WIDE, KB_WIDE))
        return 0

    lax.fori_loop(0, i // KB_WIDE, wide_body, 0)
    done = (i // KB_WIDE) * KB_WIDE
    width = KB_WIDE // 2
    while width >= 1:
        take = ((i - done) // width) * width

        @pl.when(take > 0)
        def _(done=done, width=width):
            sweep(done, width, functools.partial(past_blocks, done, width))

        done = done + take
        width //= 2

    for c in range(n_heads // 2):
        halves = []
        for h in (2 * c, 2 * c + 1):
            acc = acc_refs[h][...]
            halves.append(acc[0:HEAD_DIM] / acc[ONES_ROW:ONES_ROW + 1])
        o_ref[:, head(c)] = jnp.concatenate(halves, axis=0).T.astype(o_ref.dtype)


def _flash_attention(qt, k, vt, n_heads, batch, seq, name):
    nq = seq // TQ
    width = n_heads * LANES
    return pl.pallas_call(
        functools.partial(_flash_body, n_heads),
        grid=(batch, nq),
        in_specs=[pl.BlockSpec((None, width, TQ), lambda b, i: (b * nq + i, 0, 0)),
                  pl.BlockSpec((seq, width), lambda b, i: (b, 0)),
                  pl.BlockSpec((nq, width, TQ), lambda b, i: (b, 0, 0))],
        out_specs=pl.BlockSpec((TQ, n_heads * HEAD_DIM), lambda b, i: (b * nq + i, 0)),
        out_shape=jax.ShapeDtypeStruct((batch * seq, n_heads * HEAD_DIM), BF16),
        scratch_shapes=[pltpu.VMEM((1, TQ), F32)] * n_heads + [pltpu.VMEM((LANES, TQ), F32)] * n_heads,
        compiler_params=_cparams("parallel", "arbitrary"),
        name=name,
    )(qt, k, vt)


def _dil_body(seq, q_ref, k_ref, v_ref, o_ref, ob_ref, lse_ref, ks_ref, vt_ref):
    w = DIL_W_UNITS
    n_all = seq // w
    frow = lax.broadcasted_iota(jnp.int32, (LANES, w), 0)
    key = lax.broadcasted_iota(jnp.int32, (2 * w, w), 0)
    qry = lax.broadcasted_iota(jnp.int32, (2 * w, w), 1)
    cur_ok = (key >= w) & (key - w <= qry)
    prev_ok = (key < w) & (key >= qry)
    zeros_half = jnp.zeros((LANES - HEAD_DIM, w), F32)
    ks_ref[0:w, :] = jnp.zeros((w, LANES), BF16)
    for h in range(2):
        vt_ref[h, 0] = jnp.zeros((LANES, w), BF16)
    for bi, (_, dil) in enumerate(DIL_PAIRS):
        n_blk = seq // (dil * w)

        def rows_of(idx, dil=dil, n_blk=n_blk):
            start = idx // n_blk + (idx % n_blk) * (w * dil)
            return pl.ds(start, w, stride=dil) if dil > 1 else pl.ds(pl.multiple_of(start, w), w)

        def stage(idx, _, rows_of=rows_of):
            rows = rows_of(idx)
            ks_ref[pl.ds(pl.multiple_of((idx + 1) * w, w), w), :] = k_ref[rows, :].astype(BF16)
            v_t = v_ref[rows, :].T
            for h in range(2):
                v_h = jnp.concatenate([v_t[h * HEAD_DIM:(h + 1) * HEAD_DIM], zeros_half], axis=0)
                vt_ref[h, idx + 1] = jnp.where(frow == ONES_ROW, 1.0, v_h).astype(BF16)
            return 0

        lax.fori_loop(0, n_all, stage, 0, unroll=4)

        def one(idx, _, bi=bi, n_blk=n_blk, rows_of=rows_of):
            rows = rows_of(idx)
            q_t = q_ref[rows, :].T
            keys = ks_ref[pl.ds(pl.multiple_of(idx * w, w), 2 * w), :]
            visible = cur_ok | (prev_ok & (idx % n_blk > 0))
            outs, lses = [], []
            for h in range(2):
                q_h = jnp.where((frow // HEAD_DIM) == h, q_t, 0.0).astype(BF16)
                s = jnp.where(visible, jnp.dot(keys, q_h, preferred_element_type=F32), NEG_INF)
                m = jnp.max(s, axis=0, keepdims=True)
                p = jnp.exp2(s - m).astype(BF16)
                acc = (jnp.dot(vt_ref[h, idx], p[0:w], preferred_element_type=F32)
                       + jnp.dot(vt_ref[h, idx + 1], p[w:2 * w], preferred_element_type=F32))
                l = acc[ONES_ROW:ONES_ROW + 1]
                outs.append(acc[0:HEAD_DIM] / l)
                lses.append(jnp.broadcast_to(m + jnp.log2(l), (HEAD_DIM, w)))
            ob_ref[bi, rows, :] = jnp.concatenate(outs, axis=0).T
            lse_ref[bi, rows, :] = jnp.concatenate(lses, axis=0).T
            return 0

        lax.fori_loop(0, n_all, one, 0, unroll=8)

    def merge(c, _):
        rows = pl.ds(pl.multiple_of(c * TQ, TQ), TQ)
        ls = [lse_ref[b, rows, :] for b in range(len(DIL_PAIRS))]
        top = functools.reduce(jnp.maximum, ls)
        ws = [jnp.exp2(x - top) for x in ls]
        num = sum(wb * ob_ref[b, rows, :] for b, wb in enumerate(ws))
        o_ref[rows, :] = (num / sum(ws)).astype(o_ref.dtype)
        return 0

    lax.fori_loop(0, seq // TQ, merge, 0)


def _dil_attention(qc, kc, vc, batch, seq):
    blk = pl.BlockSpec((seq, LANES), lambda b, g: (b, g))
    nbr = len(DIL_PAIRS)
    return pl.pallas_call(
        functools.partial(_dil_body, seq),
        grid=(batch, DIL_HEADS // 2),
        in_specs=[blk, blk, blk],
        out_specs=blk,
        out_shape=jax.ShapeDtypeStruct((batch * seq, DIL_W), BF16),
        scratch_shapes=[pltpu.VMEM((nbr, seq, LANES), F32), pltpu.VMEM((nbr, seq, LANES), F32),
                        pltpu.VMEM((seq + DIL_W_UNITS, LANES), BF16),
                        pltpu.VMEM((2, seq // DIL_W_UNITS + 1, LANES, DIL_W_UNITS), BF16)],
        compiler_params=_cparams("parallel", "parallel"),
        name="dil_attn",
    )(qc, kc, vc)


def _layer_norm(y, g, b):
    mu = jnp.mean(y, axis=-1, keepdims=True)
    yc = y - mu
    var = jnp.mean(jnp.square(yc), axis=-1, keepdims=True)
    return yc * lax.rsqrt(var + LN_EPS) * g + b


def _first_argmax(rows):
    best, idx = rows[0], jnp.zeros(rows[0].shape, jnp.int32)
    for j in range(1, len(rows)):
        upd = rows[j] > best
        idx = jnp.where(upd, j, idx)
        best = jnp.where(upd, rows[j], best)
    return best, idx


def _pick(rows, idx):
    out = rows[0]
    for j in range(1, len(rows)):
        out = jnp.where(idx == j, rows[j], out)
    return out


def _top2(rows):
    v1, i1 = _first_argmax(rows)
    rest = [jnp.where(i1 == j, NEG_INF, r) for j, r in enumerate(rows)]
    v2, i2 = _first_argmax(rest)
    return v1, i1, v2, i2


def _outproj_body(oa_ref, ob_ref, oc_ref, wo_ref, x_ref, g_ref, b_ref, rwt_ref, rb_ref,
                  x1_ref, grp_ref, cnt_ref):
    i = pl.program_id(0)
    tm = x_ref.shape[0]

    @pl.when(i == 0)
    def _():
        cnt_ref[...] = jnp.zeros_like(cnt_ref)

    mix = (jnp.dot(oa_ref[...], wo_ref[0:MLA_W, :], preferred_element_type=F32)
           + jnp.dot(ob_ref[...], wo_ref[MLA_W:MLA_W + MOBA_W, :], preferred_element_type=F32)
           + jnp.dot(oc_ref[...], wo_ref[MLA_W + MOBA_W:, :], preferred_element_type=F32))
    x1 = _layer_norm(ALPHA * x_ref[...] + mix, g_ref[...], b_ref[...])
    x1_ref[:, 0:D_MODEL] = x1

    logits = _dot_nt_split(_split_bf16(rwt_ref[...]), _split_bf16(x1))
    s = jax.nn.sigmoid(logits)
    sb = s + rb_ref[...]
    s_rows = [s[e:e + 1, :] for e in range(N_EXPERTS)]
    sb_rows = [sb[e:e + 1, :] for e in range(N_EXPERTS)]
    grp = lambda rows, g: rows[g * EXP_PER_GROUP:(g + 1) * EXP_PER_GROUP]
    g_scores = []
    for g in range(N_GROUPS):
        v1, _, v2, _ = _top2(grp(sb_rows, g))
        g_scores.append(v1 + v2)
    _, g_sel = _first_argmax(g_scores)
    in_b = [_pick([grp(sb_rows, g)[j] for g in range(N_GROUPS)], g_sel) for j in range(EXP_PER_GROUP)]
    in_s = [_pick([grp(s_rows, g)[j] for g in range(N_GROUPS)], g_sel) for j in range(EXP_PER_GROUP)]
    _, j1, _, j2 = _top2(in_b)
    s1, s2 = _pick(in_s, j1), _pick(in_s, j2)
    den = s1 + s2
    g1, g2 = s1 / den, s2 / den
    gate_rows = [jnp.where(j1 == j, g1, 0.0) + jnp.where(j2 == j, g2, 0.0) for j in range(EXP_PER_GROUP)]
    gate_rows.append(jnp.zeros((LANES - EXP_PER_GROUP, tm), F32))
    x1_ref[:, D_MODEL:] = jnp.concatenate(gate_rows, axis=0).T
    grp_ref[...] = jnp.concatenate([g_sel.astype(F32), jnp.zeros((7, tm), F32)], axis=0)
    gio = lax.broadcasted_iota(jnp.int32, (N_GROUPS, tm), 0)
    cnt_ref[...] += jnp.sum(jnp.where(gio == g_sel, 1.0, 0.0), axis=1, keepdims=True)


def _outproj_ln_router(o_a, o_b, o_c, w_out_b, x2, ln_g, ln_b, rwt, rb):
    t = x2.shape[0]
    tm = TM_PROJ
    row = lambda w: pl.BlockSpec((tm, w), lambda i: (i, 0))
    full = lambda a: pl.BlockSpec(a.shape, lambda i: (0,) * a.ndim)
    return pl.pallas_call(
        _outproj_body,
        grid=(t // tm,),
        in_specs=[row(MLA_W), row(MOBA_W), row(DIL_W), full(w_out_b), row(D_MODEL), full(ln_g), full(ln_b),
                  full(rwt), full(rb)],
        out_specs=[row(ROW_W), pl.BlockSpec((8, tm), lambda i: (0, i)),
                   pl.BlockSpec((N_GROUPS, LANES), lambda i: (0, 0))],
        out_shape=[jax.ShapeDtypeStruct((t, ROW_W), F32), jax.ShapeDtypeStruct((8, t), F32),
                   jax.ShapeDtypeStruct((N_GROUPS, LANES), F32)],
        compiler_params=_cparams("arbitrary"),
        name="outproj_ln_router",
    )(o_a, o_b, o_c, w_out_b, x2, ln_g, ln_b, rwt, rb)


TN_SORT = 512
TM_ROWS = 1024


def _dest_body(grp_ref, pstart_ref, dest_ref, carry_ref):
    n = grp_ref.shape[1]

    @pl.when(pl.program_id(0) == 0)
    def _():
        carry_ref[...] = jnp.broadcast_to(pstart_ref[...], carry_ref.shape)

    before = (lax.broadcasted_iota(jnp.int32, (n, n), 0) < lax.broadcasted_iota(jnp.int32, (n, n), 1))
    before = jnp.where(before, 1.0, 0.0).astype(BF16)
    gio = lax.broadcasted_iota(jnp.int32, (N_GROUPS, n), 0)
    hit = jnp.where(gio == grp_ref[0:1, :].astype(jnp.int32), 1.0, 0.0)
    earlier = jnp.dot(hit.astype(BF16), before, preferred_element_type=F32)
    dest = jnp.sum(hit * (earlier + carry_ref[:, 0:1]), axis=0, keepdims=True)
    carry_ref[...] += jnp.sum(hit, axis=1, keepdims=True)
    dest_ref[...] = dest.astype(jnp.int32)


def _dest_rows(grp, pstart):
    t = grp.shape[1]
    return pl.pallas_call(
        _dest_body,
        grid=(t // TN_SORT,),
        in_specs=[pl.BlockSpec((8, TN_SORT), lambda i: (0, i)),
                  pl.BlockSpec((N_GROUPS, 1), lambda i: (0, 0))],
        out_specs=pl.BlockSpec((1, TN_SORT), lambda i: (0, i)),
        out_shape=jax.ShapeDtypeStruct((1, t), jnp.int32),
        scratch_shapes=[pltpu.VMEM((N_GROUPS, LANES), F32)],
        compiler_params=_cparams("arbitrary"),
        name="moe_dest",
    )(grp, pstart)


def _scatter_body(dest_ref, x_ref, xs_in_ref, xs_ref, sem):
    del xs_in_ref
    tm = x_ref.shape[0]

    def row_copy(r):
        return pltpu.make_async_copy(x_ref.at[pl.ds(r, 1)], xs_ref.at[pl.ds(dest_ref[0, r], 1)], sem)

    def issue(r, _):
        row_copy(r).start()
        return 0

    def drain(r, _):
        row_copy(r).wait()
        return 0

    lax.fori_loop(0, tm, issue, 0)
    lax.fori_loop(0, tm, drain, 0)


def _scatter_rows(dest, x1, n_rows):
    t, d = x1.shape
    xs_init = jnp.zeros((n_rows, d), x1.dtype)
    return pl.pallas_call(
        _scatter_body,
        grid=(t // TM_ROWS,),
        in_specs=[pl.BlockSpec((1, TM_ROWS), lambda i: (0, i), memory_space=pltpu.SMEM),
                  pl.BlockSpec((TM_ROWS, d), lambda i: (i, 0)),
                  pl.BlockSpec(memory_space=pl.ANY)],
        out_specs=pl.BlockSpec(memory_space=pl.ANY),
        out_shape=jax.ShapeDtypeStruct((n_rows, d), x1.dtype),
        scratch_shapes=[pltpu.SemaphoreType.DMA(())],
        input_output_aliases={2: 0},
        compiler_params=_cparams("arbitrary"),
        name="moe_scatter",
    )(dest, x1, xs_init)


def _expert_body(blk_grp_ref, nused_ref, xs_ref, w1_ref, w3_ref, w2_ref, y_ref):
    del blk_grp_ref
    i = pl.program_id(0)

    @pl.when(i < nused_ref[0])
    def _():
        xb = xs_ref[:, 0:D_MODEL].astype(BF16)
        gates = xs_ref[:, D_MODEL:]
        y = jnp.zeros(y_ref.shape, F32)
        for j in range(EXP_PER_GROUP):
            h1 = jnp.dot(xb, w1_ref[j], preferred_element_type=F32)
            h3 = jnp.dot(xb, w3_ref[j], preferred_element_type=F32)
            hb = (jax.nn.silu(h1) * h3).astype(BF16)
            y = y + jnp.dot(hb, w2_ref[j], preferred_element_type=F32) * gates[:, j:j + 1]
        y_ref[...] = y

    @pl.when(i >= nused_ref[0])
    def _():
        y_ref[...] = jnp.zeros_like(y_ref)


def _expert_ffn(blk_grp, nused, xs, w1, w3, w2, layer):
    n_rows = xs.shape[0]
    d = D_MODEL
    n_blk = n_rows // MOE_BLOCK
    grp_w = lambda r, c: pl.BlockSpec((None, None, EXP_PER_GROUP, r, c),
                                      lambda i, bg, nu: (layer, bg[i], 0, 0, 0))
    grid_spec = pltpu.PrefetchScalarGridSpec(
        num_scalar_prefetch=2,
        grid=(n_blk,),
        in_specs=[pl.BlockSpec((MOE_BLOCK, ROW_W), lambda i, bg, nu: (i, 0)),
                  grp_w(d, D_EXPERT), grp_w(d, D_EXPERT), grp_w(D_EXPERT, d)],
        out_specs=pl.BlockSpec((MOE_BLOCK, d), lambda i, bg, nu: (i, 0)),
    )
    return pl.pallas_call(
        _expert_body,
        grid_spec=grid_spec,
        out_shape=jax.ShapeDtypeStruct((n_rows, d), F32),
        compiler_params=_cparams("arbitrary"),
        name="moe_experts",
    )(blk_grp, nused, xs, w1, w3, w2)


def _combine_body(dest_ref, dest_next_ref, x1_ref, g_ref, b_ref, ys_ref, o_ref, buf_ref, sems):
    i = pl.program_id(0)
    n_steps = pl.num_programs(0)
    tm = x1_ref.shape[0]

    def row_copy(idx_ref, slot, r):
        return pltpu.make_async_copy(ys_ref.at[pl.ds(idx_ref[0, r], 1)], buf_ref.at[slot, pl.ds(r, 1)],
                                     sems.at[slot])

    def start_rows(idx_ref, slot):
        def issue(r, _):
            row_copy(idx_ref, slot, r).start()
            return 0
        lax.fori_loop(0, tm, issue, 0)

    @pl.when(i == 0)
    def _():
        start_rows(dest_ref, 0)

    @pl.when(i + 1 < n_steps)
    def _():
        start_rows(dest_next_ref, (i + 1) % 2)

    slot = i % 2

    def drain(r, _):
        row_copy(dest_ref, slot, r).wait()
        return 0

    lax.fori_loop(0, tm, drain, 0)
    o_ref[...] = _layer_norm(ALPHA * x1_ref[...] + buf_ref[slot], g_ref[...], b_ref[...])


def _combine_ln(dest, x1, ln_g, ln_b, ys):
    t = x1.shape[0]
    d = D_MODEL
    full = lambda a: pl.BlockSpec(a.shape, lambda i: (0,) * a.ndim)
    n_steps = t // TM_ROWS
    return pl.pallas_call(
        _combine_body,
        grid=(n_steps,),
        in_specs=[pl.BlockSpec((1, TM_ROWS), lambda i: (0, i), memory_space=pltpu.SMEM),
                  pl.BlockSpec((1, TM_ROWS), lambda i: (0, jnp.minimum(i + 1, n_steps - 1)),
                               memory_space=pltpu.SMEM),
                  pl.BlockSpec((TM_ROWS, d), lambda i: (i, 0)), full(ln_g), full(ln_b),
                  pl.BlockSpec(memory_space=pl.ANY)],
        out_specs=pl.BlockSpec((TM_ROWS, d), lambda i: (i, 0)),
        out_shape=jax.ShapeDtypeStruct((t, d), F32),
        scratch_shapes=[pltpu.VMEM((2, TM_ROWS, d), F32), pltpu.SemaphoreType.DMA((2,))],
        compiler_params=_cparams("arbitrary"),
        name="moe_combine_ln",
    )(dest, dest, x1, ln_g, ln_b, ys)


def _moe_layer(x1, grp, counts, w1, w3, w2, layer, ln_g, ln_b):
    t = x1.shape[0]
    n_rows = t + N_GROUPS * MOE_BLOCK
    n_blk = n_rows // MOE_BLOCK
    cnt = counts[:, 0].astype(jnp.int32)
    padded = (cnt + MOE_BLOCK - 1) // MOE_BLOCK * MOE_BLOCK
    pend = jnp.cumsum(padded)
    pstart = (pend - padded).astype(F32).reshape(N_GROUPS, 1)
    blk_row0 = jnp.arange(n_blk, dtype=jnp.int32) * MOE_BLOCK
    blk_grp = jnp.minimum(jnp.sum((pend[None, :] <= blk_row0[:, None]).astype(jnp.int32), axis=1), N_GROUPS - 1)
    nused = (pend[-1:] // MOE_BLOCK).astype(jnp.int32)
    dest = _dest_rows(grp, pstart)
    xs = _scatter_rows(dest, x1, n_rows)
    ys = _expert_ffn(blk_grp, nused, xs, w1, w3, w2, layer)
    return _combine_ln(dest, x1, ln_g, ln_b, ys)


def _rope_tables(seq):
    def tab(dim):
        inv = ROPE_THETA ** (-jnp.arange(0, dim, 2, dtype=F32) / dim)
        ang = jnp.arange(seq, dtype=F32)[:, None] * inv[None, :]
        return jnp.cos(ang), jnp.sin(ang)

    c, s = tab(HEAD_DIM)
    cos64 = jnp.tile(jnp.concatenate([c, c], axis=1), (1, LANES // HEAD_DIM))
    sin64 = jnp.tile(jnp.concatenate([-s, s], axis=1), (1, LANES // HEAD_DIM))
    c, s = tab(MLA_ROPE)
    pad_l, pad_r = MLA_ROPE_LANE, LANES - MLA_ROPE_LANE - MLA_ROPE
    cosm = jnp.concatenate([jnp.ones((seq, pad_l), F32), c, c, jnp.ones((seq, pad_r), F32)], axis=1)
    sinm = jnp.concatenate([jnp.zeros((seq, pad_l), F32), -s, s, jnp.zeros((seq, pad_r), F32)], axis=1)
    return cos64, sin64, cosm, sinm


def _prep_mixer_weights(w_in, q_norm, w_uq, kv_norm, w_ukv):
    d = w_in.shape[0]
    lat = MLA_Q_LORA + MLA_KV_LORA
    w_in_p = jnp.concatenate([
        w_in[:, :lat], jnp.zeros((d, MLA_ROPE_LANE), F32), w_in[:, lat:lat + MLA_ROPE],
        jnp.zeros((d, LANES - MLA_ROPE_LANE - MLA_ROPE), F32), w_in[:, lat + MLA_ROPE:]], axis=1).astype(BF16)
    wq = w_uq.reshape(MLA_Q_LORA, MLA_HEADS, MLA_NOPE + MLA_ROPE)
    wuq_p = jnp.pad(wq, ((0, 0), (0, 0), (0, MLA_HEAD_PAD - MLA_NOPE - MLA_ROPE)))
    wuq_p = wuq_p.reshape(MLA_Q_LORA, MLA_HEADS * MLA_HEAD_PAD).astype(BF16)
    wkv = w_ukv.reshape(MLA_KV_LORA, MLA_HEADS, MLA_NOPE + MLA_V)
    wk = jnp.pad(wkv[:, :, :MLA_NOPE], ((0, 0), (0, 0), (0, MLA_HEAD_PAD - MLA_NOPE)))
    wukv_p = jnp.concatenate([wk.reshape(MLA_KV_LORA, MLA_HEADS * MLA_HEAD_PAD),
                              wkv[:, :, MLA_NOPE:].reshape(MLA_KV_LORA, MLA_W)], axis=1).astype(BF16)
    return w_in_p, q_norm.reshape(1, -1), wuq_p, kv_norm.reshape(1, -1), wukv_p


def _mixer_heads(x2, w_in, q_norm, w_uq, kv_norm, w_ukv, tabs, batch, seq):
    w_in_p, qn, wuq_p, kvn, wukv_p = _prep_mixer_weights(w_in, q_norm, w_uq, kv_norm, w_ukv)
    qm, km, vm, qb, kb, vb, qc, kc, vc = _inproj(x2, w_in_p, qn, kvn, wuq_p, wukv_p, tabs, seq)
    o_a = _flash_attention(qm, km, vm, MLA_HEADS, batch, seq, "mla_attn")
    o_b = _flash_attention(qb, kb, vb, MOBA_HEADS, batch, seq, "moba_attn")
    o_c = _dil_attention(qc, kc, vc, batch, seq)
    return o_a, o_b, o_c


def kernel(x, w_in, mla_q_norm, mla_w_uq, mla_kv_norm, mla_w_ukv, w_out, ln1_g, ln1_b, router_w, router_b, moe_w1, moe_w3, moe_w2, ln2_g, ln2_b):
    batch, seq, d = x.shape
    tabs = _rope_tables(seq)
    x2 = x.reshape(batch * seq, d)
    rwt = router_w.T
    rb = router_b.reshape(N_EXPERTS, 1)
    by_group = lambda w: w.astype(BF16).reshape(DEPTH, N_GROUPS, EXP_PER_GROUP, *w.shape[2:])
    w1g, w3g, w2g = by_group(moe_w1), by_group(moe_w3), by_group(moe_w2)
    for l in range(DEPTH):
        o_a, o_b, o_c = _mixer_heads(x2, w_in[l], mla_q_norm[l], mla_w_uq[l], mla_kv_norm[l], mla_w_ukv[l],
                                     tabs, batch, seq)
        x1, route, counts = _outproj_ln_router(o_a, o_b, o_c, w_out[l].astype(BF16), x2,
                                               ln1_g[l].reshape(1, d), ln1_b[l].reshape(1, d), rwt, rb)
        x2 = _moe_layer(x1, route, counts, w1g, w3g, w2g, l,
                        ln2_g[l].reshape(1, d), ln2_b[l].reshape(1, d))
    return x2.reshape(batch, seq, d)
```

```python
import functools

import jax
import jax.numpy as jnp
import numpy as np
from jax import lax
from jax.experimental import pallas as pl
from jax.experimental.pallas import tpu as pltpu

F32 = jnp.float32
BF16 = jnp.bfloat16
LANES = 128
NEG_INF = float("-inf")
VMEM_LIMIT_BYTES = 56 * 1024 * 1024

D_MODEL = 1024
DEPTH = 2
HEAD_DIM = 64
ROPE_THETA = 10000.0
MLA_HEADS = 4
MLA_NOPE = 64
MLA_ROPE = 32
MLA_V = 64
MLA_Q_LORA = 256
MLA_KV_LORA = 128
MOBA_HEADS = 6
MOBA_BLOCK = 256
MOBA_TOPK = 3
DIL_HEADS = 6
DIL_PAIRS = ((128, 1), (512, 4), (2048, 16))
MLA_W = MLA_HEADS * MLA_V
MOBA_W = MOBA_HEADS * HEAD_DIM
DIL_W = DIL_HEADS * HEAD_DIM
N_EXPERTS = 64
N_GROUPS = 8
EXP_PER_GROUP = N_EXPERTS // N_GROUPS
TOP_K = 2
D_EXPERT = 256
MOE_BLOCK = 512
ALPHA = (2 * DEPTH) ** 0.25
LN_EPS = 1e-5
RMS_EPS = 1e-6

SEG_A = 512
SEG_B = 3 * MOBA_W
SEG_C = 3 * DIL_W
N_IN_PAD = SEG_A + SEG_B + SEG_C
MLA_HEAD_PAD = 128
MLA_ROPE_LANE = 64

TILE_ROWS = 8
PACK_ROWS = D_MODEL // 2 // 128
TM_PROJ = 512
TQ = 256
KB_WIDE = 4
DIL_W_UNITS = 128
LOG2_E = 1.4426950408889634
ONES_ROW = HEAD_DIM
SEL_LANE = HEAD_DIM
MASK_BIG = 2.0 ** 100


def _cparams(*sem):
    return pltpu.CompilerParams(dimension_semantics=sem, vmem_limit_bytes=VMEM_LIMIT_BYTES)


def _rope(x, cos, sin_signed, half):
    lane = lax.broadcasted_iota(jnp.int32, x.shape, 1)
    first = ((lane // half) % 2) == 0
    rot = jnp.where(first, pltpu.roll(x, LANES - half, 1), pltpu.roll(x, half, 1))
    return x * cos + rot * sin_signed


def _split_bf16(x):
    hi = x.astype(BF16)
    return hi, (x - hi.astype(F32)).astype(BF16)


def _dot_nt_split(a_parts, b_parts):
    dn = (((1,), (1,)), ((), ()))
    dot = lambda x, y: lax.dot_general(x, y, dn, preferred_element_type=F32)
    (a_hi, a_lo), (b_hi, b_lo) = a_parts, b_parts
    return dot(a_hi, b_hi) + (dot(a_hi, b_lo) + dot(a_lo, b_hi))


def _rms(x, g):
    return x * lax.rsqrt(jnp.mean(jnp.square(x), axis=-1, keepdims=True) + RMS_EPS) * g


def _inproj_body(seq_blocks, x_ref, w_ref, qn_ref, kvn_ref, wuq_ref, wukv_ref,
                 cos64_ref, sin64_ref, cosm_ref, sinm_ref,
                 qm_ref, km_ref, vm_ref, qb_ref, kb_ref, vb_ref, qc_ref, kc_ref, vc_ref,
                 kmean_ref):
    i = pl.program_id(0)
    tm = x_ref.shape[0]
    n_kb = kmean_ref.shape[0]
    sblk = i % seq_blocks

    @pl.when(sblk == 0)
    def _():
        kmean_ref[...] = jnp.zeros_like(kmean_ref)

    xb = x_ref[...].astype(BF16)
    cos64, sin64 = cos64_ref[...], sin64_ref[...]
    cosm, sinm = cosm_ref[...], sinm_ref[...]

    h_a = jnp.dot(xb, w_ref[:, 0:SEG_A], preferred_element_type=F32)
    cq = _rms(h_a[:, 0:MLA_Q_LORA], qn_ref[...]).astype(BF16)
    ckv = _rms(h_a[:, MLA_Q_LORA:MLA_Q_LORA + MLA_KV_LORA], kvn_ref[...]).astype(BF16)
    kr = _rope(h_a[:, MLA_Q_LORA + MLA_KV_LORA:SEG_A], cosm, sinm, MLA_ROPE // 2)
    q_m = jnp.dot(cq, wuq_ref[...], preferred_element_type=F32)
    kv_m = jnp.dot(ckv, wukv_ref[...], preferred_element_type=F32)
    mla_scale = (MLA_NOPE + MLA_ROPE) ** -0.5 * LOG2_E
    lane_t = lax.broadcasted_iota(jnp.int32, (tm, LANES), 1)
    row_t = lax.broadcasted_iota(jnp.int32, (LANES, tm), 0)
    zeros_half = jnp.zeros((LANES - HEAD_DIM, tm), F32)

    def put_t(ref, h, val_t):
        for b in range(tm // TQ):
            ref[b, h * LANES:(h + 1) * LANES, :] = val_t[:, b * TQ:(b + 1) * TQ].astype(ref.dtype)

    def v_heads_t(pair):
        pt = pair.T
        return [jnp.where(row_t == ONES_ROW, 1.0, jnp.concatenate([pt[s * HEAD_DIM:(s + 1) * HEAD_DIM], zeros_half], 0))
                for s in range(2)]

    for h in range(MLA_HEADS):
        sl = slice(h * MLA_HEAD_PAD, (h + 1) * MLA_HEAD_PAD)
        put_t(qm_ref, h, (_rope(q_m[:, sl], cosm, sinm, MLA_ROPE // 2) * mla_scale).T)
        km_ref[:, sl] = (kv_m[:, sl] + kr).astype(BF16)
    for c in range(MLA_HEADS // 2):
        v_pair = kv_m[:, (MLA_HEADS + c) * LANES:(MLA_HEADS + c + 1) * LANES]
        for s, v_t in enumerate(v_heads_t(v_pair)):
            put_t(vm_ref, 2 * c + s, v_t)

    h_b = jnp.dot(xb, w_ref[:, SEG_A:SEG_A + SEG_B], preferred_element_type=F32)
    scale = HEAD_DIM ** -0.5
    lane = lax.broadcasted_iota(jnp.int32, (n_kb, LANES), 1)
    pos = sblk * tm + lax.broadcasted_iota(jnp.int32, (1, tm), 1)
    q_blk = pos // MOBA_BLOCK
    jio = lax.broadcasted_iota(jnp.int32, (n_kb, tm), 0)
    elig = jio < q_blk
    row_blk = (sblk * tm + lax.broadcasted_iota(jnp.int32, (tm, LANES), 0)) // MOBA_BLOCK
    blk_lane = lane_t == SEL_LANE + row_blk
    first_head = lane_t < HEAD_DIM

    def split(pair):
        return (jnp.where(first_head, pair, 0.0), jnp.where(first_head, pltpu.roll(pair, HEAD_DIM, 1), 0.0))

    for c in range(MOBA_W // LANES):
        sl = slice(c * LANES, (c + 1) * LANES)
        q = _rope(h_b[:, sl], cos64, sin64, HEAD_DIM // 2)
        k = _rope(h_b[:, MOBA_W + c * LANES:MOBA_W + (c + 1) * LANES], cos64, sin64, HEAD_DIM // 2)
        km = kmean_ref[:, sl]
        kb_row = lax.broadcasted_iota(jnp.int32, (n_kb, LANES), 0)
        for b in range(tm // MOBA_BLOCK):
            mean_b = jnp.mean(k[b * MOBA_BLOCK:(b + 1) * MOBA_BLOCK], axis=0, keepdims=True)
            km = jnp.where(kb_row == sblk * (tm // MOBA_BLOCK) + b, mean_b, km)
        kmean_ref[:, sl] = km
        q_t = (q * (scale * LOG2_E)).T
        k_heads = split(k)
        q_parts = _split_bf16(q)
        v_t = v_heads_t(h_b[:, 2 * MOBA_W + c * LANES:2 * MOBA_W + (c + 1) * LANES])
        for half in range(2):
            hs = slice((2 * c + half) * LANES, (2 * c + half + 1) * LANES)
            in_head = (lane // HEAD_DIM) == half
            gate = _dot_nt_split(_split_bf16(jnp.where(in_head, km, 0.0)), q_parts)
            gate = jnp.where(elig, gate, NEG_INF)
            q_rows = [q_t[half * HEAD_DIM:(half + 1) * HEAD_DIM]]
            for j in range(n_kb):
                gj = gate[j:j + 1, :]
                beats = (gate > gj) | ((gate == gj) & (jio < j))
                cnt = jnp.sum(beats.astype(F32), axis=0, keepdims=True)
                keep = ((cnt < MOBA_TOPK) & elig[j:j + 1, :]) | (q_blk == j)
                q_rows.append(jnp.where(keep, 0.0, -MASK_BIG))
            q_rows.append(jnp.zeros((LANES - SEL_LANE - n_kb, tm), F32))
            put_t(qb_ref, 2 * c + half, jnp.concatenate(q_rows, axis=0))
            kb_ref[:, hs] = jnp.where(blk_lane, 1.0, k_heads[half]).astype(BF16)
            put_t(vb_ref, 2 * c + half, v_t[half])

    h_c = jnp.dot(xb, w_ref[:, SEG_A + SEG_B:], preferred_element_type=F32)
    for c in range(DIL_W // LANES):
        sl = slice(c * LANES, (c + 1) * LANES)
        qc_ref[:, sl] = _rope(h_c[:, sl], cos64, sin64, HEAD_DIM // 2) * (scale * LOG2_E)
        kc_ref[:, sl] = _rope(h_c[:, DIL_W + c * LANES:DIL_W + (c + 1) * LANES], cos64, sin64, HEAD_DIM // 2)
    vc_ref[...] = h_c[:, 2 * DIL_W:]


def _inproj(x2, w_in_p, qn, kvn, wuq_p, wukv_p, tabs, seq):
    t = x2.shape[0]
    tm = TM_PROJ
    seq_blocks = seq // tm
    n_kb = seq // MOBA_BLOCK
    row = lambda w: pl.BlockSpec((tm, w), lambda i: (i, 0))
    full = lambda a: pl.BlockSpec(a.shape, lambda i: (0,) * a.ndim)
    tab = pl.BlockSpec((tm, LANES), lambda i: (i % seq_blocks, 0))
    out_shapes = [
        jax.ShapeDtypeStruct((t // TQ, MLA_HEADS * LANES, TQ), BF16),
        jax.ShapeDtypeStruct((t, MLA_HEADS * MLA_HEAD_PAD), BF16),
        jax.ShapeDtypeStruct((t // TQ, MLA_HEADS * LANES, TQ), BF16),
        jax.ShapeDtypeStruct((t // TQ, MOBA_HEADS * LANES, TQ), BF16),
        jax.ShapeDtypeStruct((t, MOBA_HEADS * LANES), BF16),
        jax.ShapeDtypeStruct((t // TQ, MOBA_HEADS * LANES, TQ), BF16),
        jax.ShapeDtypeStruct((t, DIL_W), F32),
        jax.ShapeDtypeStruct((t, DIL_W), F32),
        jax.ShapeDtypeStruct((t, DIL_W), F32),
    ]
    return pl.pallas_call(
        functools.partial(_inproj_body, seq_blocks),
        grid=(t // tm,),
        in_specs=[row(D_MODEL), full(w_in_p), full(qn), full(kvn), full(wuq_p), full(wukv_p),
                  tab, tab, tab, tab],
        out_specs=[row(s.shape[1]) if len(s.shape) == 2 else
                   pl.BlockSpec((tm // TQ,) + s.shape[1:], lambda i: (i, 0, 0)) for s in out_shapes],
        out_shape=out_shapes,
        scratch_shapes=[pltpu.VMEM((n_kb, MOBA_W), F32)],
        compiler_params=_cparams("arbitrary"),
        name="inproj",
    )(x2, w_in_p, qn, kvn, wuq_p, wukv_p, *tabs)


def _flash_body(n_heads, qt_ref, k_ref, vt_ref, o_ref, *state):
    i = pl.program_id(1)
    tq = qt_ref.shape[1]

    def head(h):
        return slice(h * LANES, (h + 1) * LANES)

    m_refs, acc_refs = state[:n_heads], state[n_heads:]
    key_le_query = (lax.broadcasted_iota(jnp.int32, (tq, tq), 0) <= lax.broadcasted_iota(jnp.int32, (tq, tq), 1))

    def scores(j, n, h):
        rows = pl.ds(pl.multiple_of(j * tq, tq), n * tq)
        return jnp.dot(k_ref[rows, head(h)], qt_ref[head(h), :], preferred_element_type=F32)

    def pv(j, n, h, p):
        return sum(jnp.dot(vt_ref[j + b, head(h), :], p[b * tq:(b + 1) * tq], preferred_element_type=F32)
                   for b in range(n))

    def own_block(h, s):
        s = jnp.where(key_le_query, s, NEG_INF)
        m = jnp.max(s, axis=0, keepdims=True)
        acc_refs[h][...] = pv(i, 1, h, jnp.exp2(s - m).astype(BF16))
        m_refs[h][...] = m

    def past_blocks(j, n, h, s):
        m_old = m_refs[h][...]
        m_new = jnp.maximum(m_old, jnp.max(s, axis=0, keepdims=True))
        p = jnp.exp2(s - m_new).astype(BF16)
        acc_refs[h][...] = jnp.exp2(m_old - m_new) * acc_refs[h][...] + pv(j, n, h, p)
        m_refs[h][...] = m_new

    def sweep(j, n, update):
        s_next = scores(j, n, 0)
        for h in range(n_heads):
            s = s_next
            if h + 1 < n_heads:
                s_next = scores(j, n, h + 1)
            update(h, s)

    sweep(i, 1, own_block)

    def wide_body(jj, _):
        sweep(jj * KB_WIDE, KB_WIDE, functools.partial(past_blocks, jj * KB_WIDE, KB_WIDE))
        return 0

    lax.fori_loop(0, i // KB_WIDE, wide_body, 0)
    done = (i // KB_WIDE) * KB_WIDE
    width = KB_WIDE // 2
    while width >= 1:
        take = ((i - done) // width) * width

        @pl.when(take > 0)
        def _(done=done, width=width):
            sweep(done, width, functools.partial(past_blocks, done, width))

        done = done + take
        width //= 2

    for c in range(n_heads // 2):
        halves = []
        for h in (2 * c, 2 * c + 1):
            acc = acc_refs[h][...]
            halves.append(acc[0:HEAD_DIM] / acc[ONES_ROW:ONES_ROW + 1])
        o_ref[:, head(c)] = jnp.concatenate(halves, axis=0).T.astype(o_ref.dtype)


def _flash_attention(qt, k, vt, n_heads, batch, seq, name):
    nq = seq // TQ
    width = n_heads * LANES
    return pl.pallas_call(
        functools.partial(_flash_body, n_heads),
        grid=(batch, nq),
        in_specs=[pl.BlockSpec((None, width, TQ), lambda b, i: (b * nq + i, 0, 0)),
                  pl.BlockSpec((seq, width), lambda b, i: (b, 0)),
                  pl.BlockSpec((nq, width, TQ), lambda b, i: (b, 0, 0))],
        out_specs=pl.BlockSpec((TQ, n_heads * HEAD_DIM), lambda b, i: (b * nq + i, 0)),
        out_shape=jax.ShapeDtypeStruct((batch * seq, n_heads * HEAD_DIM), BF16),
        scratch_shapes=[pltpu.VMEM((1, TQ), F32)] * n_heads + [pltpu.VMEM((LANES, TQ), F32)] * n_heads,
        compiler_params=_cparams("parallel", "arbitrary"),
        name=name,
    )(qt, k, vt)


def _dil_body(seq, q_ref, k_ref, v_ref, o_ref, ob_ref, lse_ref, ks_ref, vt_ref):
    w = DIL_W_UNITS
    n_all = seq // w
    frow = lax.broadcasted_iota(jnp.int32, (LANES, w), 0)
    key = lax.broadcasted_iota(jnp.int32, (2 * w, w), 0)
    qry = lax.broadcasted_iota(jnp.int32, (2 * w, w), 1)
    cur_ok = (key >= w) & (key - w <= qry)
    prev_ok = (key < w) & (key >= qry)
    zeros_half = jnp.zeros((LANES - HEAD_DIM, w), F32)
    ks_ref[0:w, :] = jnp.zeros((w, LANES), BF16)
    for h in range(2):
        vt_ref[h, 0] = jnp.zeros((LANES, w), BF16)
    for bi, (_, dil) in enumerate(DIL_PAIRS):
        n_blk = seq // (dil * w)

        def rows_of(idx, dil=dil, n_blk=n_blk):
            start = idx // n_blk + (idx % n_blk) * (w * dil)
            return pl.ds(start, w, stride=dil) if dil > 1 else pl.ds(pl.multiple_of(start, w), w)

        def stage(idx, _, rows_of=rows_of):
            rows = rows_of(idx)
            ks_ref[pl.ds(pl.multiple_of((idx + 1) * w, w), w), :] = k_ref[rows, :].astype(BF16)
            v_t = v_ref[rows, :].T
            for h in range(2):
                v_h = jnp.concatenate([v_t[h * HEAD_DIM:(h + 1) * HEAD_DIM], zeros_half], axis=0)
                vt_ref[h, idx + 1] = jnp.where(frow == ONES_ROW, 1.0, v_h).astype(BF16)
            return 0

        lax.fori_loop(0, n_all, stage, 0, unroll=4)

        def one(idx, _, bi=bi, n_blk=n_blk, rows_of=rows_of):
            rows = rows_of(idx)
            q_t = q_ref[rows, :].T
            keys = ks_ref[pl.ds(pl.multiple_of(idx * w, w), 2 * w), :]
            visible = cur_ok | (prev_ok & (idx % n_blk > 0))
            outs, lses = [], []
            for h in range(2):
                q_h = jnp.where((frow // HEAD_DIM) == h, q_t, 0.0).astype(BF16)
                s = jnp.where(visible, jnp.dot(keys, q_h, preferred_element_type=F32), NEG_INF)
                m = jnp.max(s, axis=0, keepdims=True)
                p = jnp.exp2(s - m).astype(BF16)
                acc = (jnp.dot(vt_ref[h, idx], p[0:w], preferred_element_type=F32)
                       + jnp.dot(vt_ref[h, idx + 1], p[w:2 * w], preferred_element_type=F32))
                l = acc[ONES_ROW:ONES_ROW + 1]
                outs.append(acc[0:HEAD_DIM] / l)
                lses.append(jnp.broadcast_to(m + jnp.log2(l), (HEAD_DIM, w)))
            ob_ref[bi, rows, :] = jnp.concatenate(outs, axis=0).T
            lse_ref[bi, rows, :] = jnp.concatenate(lses, axis=0).T
            return 0

        lax.fori_loop(0, n_all, one, 0, unroll=8)

    def merge(c, _):
        rows = pl.ds(pl.multiple_of(c * TQ, TQ), TQ)
        ls = [lse_ref[b, rows, :] for b in range(len(DIL_PAIRS))]
        top = functools.reduce(jnp.maximum, ls)
        ws = [jnp.exp2(x - top) for x in ls]
        num = sum(wb * ob_ref[b, rows, :] for b, wb in enumerate(ws))
        o_ref[rows, :] = (num / sum(ws)).astype(o_ref.dtype)
        return 0

    lax.fori_loop(0, seq // TQ, merge, 0)


def _dil_attention(qc, kc, vc, batch, seq):
    blk = pl.BlockSpec((seq, LANES), lambda b, g: (b, g))
    nbr = len(DIL_PAIRS)
    return pl.pallas_call(
        functools.partial(_dil_body, seq),
        grid=(batch, DIL_HEADS // 2),
        in_specs=[blk, blk, blk],
        out_specs=blk,
        out_shape=jax.ShapeDtypeStruct((batch * seq, DIL_W), BF16),
        scratch_shapes=[pltpu.VMEM((nbr, seq, LANES), F32), pltpu.VMEM((nbr, seq, LANES), F32),
                        pltpu.VMEM((seq + DIL_W_UNITS, LANES), BF16),
                        pltpu.VMEM((2, seq // DIL_W_UNITS + 1, LANES, DIL_W_UNITS), BF16)],
        compiler_params=_cparams("parallel", "parallel"),
        name="dil_attn",
    )(qc, kc, vc)


def _layer_norm(y, g, b):
    mu = jnp.mean(y, axis=-1, keepdims=True)
    yc = y - mu
    var = jnp.mean(jnp.square(yc), axis=-1, keepdims=True)
    return yc * lax.rsqrt(var + LN_EPS) * g + b


def _first_argmax(rows):
    best, idx = rows[0], jnp.zeros(rows[0].shape, jnp.int32)
    for j in range(1, len(rows)):
        upd = rows[j] > best
        idx = jnp.where(upd, j, idx)
        best = jnp.where(upd, rows[j], best)
    return best, idx


def _pick(rows, idx):
    out = rows[0]
    for j in range(1, len(rows)):
        out = jnp.where(idx == j, rows[j], out)
    return out


def _top2(rows):
    v1, i1 = _first_argmax(rows)
    rest = [jnp.where(i1 == j, NEG_INF, r) for j, r in enumerate(rows)]
    v2, i2 = _first_argmax(rest)
    return v1, i1, v2, i2


def _outproj_body(oa_ref, ob_ref, oc_ref, wo_ref, x_ref, g_ref, b_ref, rwt_ref, rb_ref,
                  x1_ref, xp_ref, grp_ref, cnt_ref):
    i = pl.program_id(0)
    tm = x_ref.shape[0]

    @pl.when(i == 0)
    def _():
        cnt_ref[...] = jnp.zeros_like(cnt_ref)

    mix = (jnp.dot(oa_ref[...], wo_ref[0:MLA_W, :], preferred_element_type=F32)
           + jnp.dot(ob_ref[...], wo_ref[MLA_W:MLA_W + MOBA_W, :], preferred_element_type=F32)
           + jnp.dot(oc_ref[...], wo_ref[MLA_W + MOBA_W:, :], preferred_element_type=F32))
    x1 = _layer_norm(ALPHA * x_ref[...] + mix, g_ref[...], b_ref[...])
    x1_ref[...] = x1
    bits = lax.bitcast_convert_type(x1.astype(BF16).astype(F32), jnp.uint32)
    words = bits[:, 0:D_MODEL // 2] | (bits[:, D_MODEL // 2:] >> 16)
    for r in range(PACK_ROWS):
        xp_ref[pl.ds(r, tm, stride=TILE_ROWS), :] = words[:, r * LANES:(r + 1) * LANES]
    for r in range(PACK_ROWS + 1, TILE_ROWS):
        xp_ref[pl.ds(r, tm, stride=TILE_ROWS), :] = jnp.zeros((tm, LANES), jnp.uint32)

    logits = _dot_nt_split(_split_bf16(rwt_ref[...]), _split_bf16(x1))
    s = jax.nn.sigmoid(logits)
    sb = s + rb_ref[...]
    s_rows = [s[e:e + 1, :] for e in range(N_EXPERTS)]
    sb_rows = [sb[e:e + 1, :] for e in range(N_EXPERTS)]
    grp = lambda rows, g: rows[g * EXP_PER_GROUP:(g + 1) * EXP_PER_GROUP]
    g_scores = []
    for g in range(N_GROUPS):
        v1, _, v2, _ = _top2(grp(sb_rows, g))
        g_scores.append(v1 + v2)
    _, g_sel = _first_argmax(g_scores)
    in_b = [_pick([grp(sb_rows, g)[j] for g in range(N_GROUPS)], g_sel) for j in range(EXP_PER_GROUP)]
    in_s = [_pick([grp(s_rows, g)[j] for g in range(N_GROUPS)], g_sel) for j in range(EXP_PER_GROUP)]
    _, j1, _, j2 = _top2(in_b)
    s1, s2 = _pick(in_s, j1), _pick(in_s, j2)
    den = s1 + s2
    g1, g2 = s1 / den, s2 / den
    gate_rows = [jnp.where(j1 == j, g1, 0.0) + jnp.where(j2 == j, g2, 0.0) for j in range(EXP_PER_GROUP)]
    gate_rows.append(jnp.zeros((LANES - EXP_PER_GROUP, tm), F32))
    xp_ref[pl.ds(PACK_ROWS, tm, stride=TILE_ROWS), :] = lax.bitcast_convert_type(
        jnp.concatenate(gate_rows, axis=0).T, jnp.uint32)
    grp_ref[...] = jnp.concatenate([g_sel.astype(F32), jnp.zeros((7, tm), F32)], axis=0)
    gio = lax.broadcasted_iota(jnp.int32, (N_GROUPS, tm), 0)
    cnt_ref[...] += jnp.sum(jnp.where(gio == g_sel, 1.0, 0.0), axis=1, keepdims=True)


def _outproj_ln_router(o_a, o_b, o_c, w_out_b, x2, ln_g, ln_b, rwt, rb):
    t = x2.shape[0]
    tm = TM_PROJ
    row = lambda w: pl.BlockSpec((tm, w), lambda i: (i, 0))
    full = lambda a: pl.BlockSpec(a.shape, lambda i: (0,) * a.ndim)
    return pl.pallas_call(
        _outproj_body,
        grid=(t // tm,),
        in_specs=[row(MLA_W), row(MOBA_W), row(DIL_W), full(w_out_b), row(D_MODEL), full(ln_g), full(ln_b),
                  full(rwt), full(rb)],
        out_specs=[row(D_MODEL), pl.BlockSpec((tm * TILE_ROWS, LANES), lambda i: (i, 0)),
                   pl.BlockSpec((8, tm), lambda i: (0, i)),
                   pl.BlockSpec((N_GROUPS, LANES), lambda i: (0, 0))],
        out_shape=[jax.ShapeDtypeStruct((t, D_MODEL), F32),
                   jax.ShapeDtypeStruct((t * TILE_ROWS, LANES), jnp.uint32),
                   jax.ShapeDtypeStruct((8, t), F32),
                   jax.ShapeDtypeStruct((N_GROUPS, LANES), F32)],
        compiler_params=_cparams("arbitrary"),
        name="outproj_ln_router",
    )(o_a, o_b, o_c, w_out_b, x2, ln_g, ln_b, rwt, rb)


TN_SORT = 512
TM_ROWS = 1024


def _dest_body(grp_ref, pstart_ref, dest_ref, carry_ref):
    n = grp_ref.shape[1]

    @pl.when(pl.program_id(0) == 0)
    def _():
        carry_ref[...] = jnp.broadcast_to(pstart_ref[...], carry_ref.shape)

    before = (lax.broadcasted_iota(jnp.int32, (n, n), 0) < lax.broadcasted_iota(jnp.int32, (n, n), 1))
    before = jnp.where(before, 1.0, 0.0).astype(BF16)
    gio = lax.broadcasted_iota(jnp.int32, (N_GROUPS, n), 0)
    hit = jnp.where(gio == grp_ref[0:1, :].astype(jnp.int32), 1.0, 0.0)
    earlier = jnp.dot(hit.astype(BF16), before, preferred_element_type=F32)
    dest = jnp.sum(hit * (earlier + carry_ref[:, 0:1]), axis=0, keepdims=True)
    carry_ref[...] += jnp.sum(hit, axis=1, keepdims=True)
    dest_ref[...] = dest.astype(jnp.int32)


def _dest_rows(grp, pstart):
    t = grp.shape[1]
    return pl.pallas_call(
        _dest_body,
        grid=(t // TN_SORT,),
        in_specs=[pl.BlockSpec((8, TN_SORT), lambda i: (0, i)),
                  pl.BlockSpec((N_GROUPS, 1), lambda i: (0, 0))],
        out_specs=pl.BlockSpec((1, TN_SORT), lambda i: (0, i)),
        out_shape=jax.ShapeDtypeStruct((1, t), jnp.int32),
        scratch_shapes=[pltpu.VMEM((N_GROUPS, LANES), F32)],
        compiler_params=_cparams("arbitrary"),
        name="moe_dest",
    )(grp, pstart)


def _tile(row):
    return pl.ds(pl.multiple_of(row * TILE_ROWS, TILE_ROWS), TILE_ROWS)


def _tile_row(r, n):
    return pl.ds(r, n, stride=TILE_ROWS)


def _scatter_body(dest_ref, x_ref, xs_in_ref, xs_ref, sem):
    del xs_in_ref
    tm = x_ref.shape[0] // TILE_ROWS

    def tile_copy(r):
        return pltpu.make_async_copy(x_ref.at[_tile(r)], xs_ref.at[_tile(dest_ref[0, r])], sem)

    def issue(r, _):
        tile_copy(r).start()
        return 0

    def drain(r, _):
        tile_copy(r).wait()
        return 0

    lax.fori_loop(0, tm, issue, 0)
    lax.fori_loop(0, tm, drain, 0)


def _scatter_rows(dest, xp, n_rows):
    t = xp.shape[0] // TILE_ROWS
    xs_init = jnp.zeros((n_rows * TILE_ROWS, LANES), xp.dtype)
    return pl.pallas_call(
        _scatter_body,
        grid=(t // TM_ROWS,),
        in_specs=[pl.BlockSpec((1, TM_ROWS), lambda i: (0, i), memory_space=pltpu.SMEM),
                  pl.BlockSpec((TM_ROWS * TILE_ROWS, LANES), lambda i: (i, 0)),
                  pl.BlockSpec(memory_space=pl.ANY)],
        out_specs=pl.BlockSpec(memory_space=pl.ANY),
        out_shape=jax.ShapeDtypeStruct(xs_init.shape, xp.dtype),
        scratch_shapes=[pltpu.SemaphoreType.DMA(())],
        input_output_aliases={2: 0},
        compiler_params=_cparams("arbitrary"),
        name="moe_scatter",
    )(dest, xp, xs_init)


def _expert_body(blk_grp_ref, nused_ref, xs_ref, w1_ref, w3_ref, w2_ref, y_ref):
    del blk_grp_ref
    i = pl.program_id(0)
    rows = xs_ref.shape[0] // TILE_ROWS

    @pl.when(i < nused_ref[0])
    def _():
        words = jnp.concatenate([xs_ref[_tile_row(r, rows), :] for r in range(PACK_ROWS)], axis=1)
        unpack = lambda bits: lax.bitcast_convert_type(bits, F32).astype(BF16)
        xb = jnp.concatenate([unpack(words & jnp.uint32(0xFFFF0000)), unpack(words << 16)], axis=1)
        gates = lax.bitcast_convert_type(xs_ref[_tile_row(PACK_ROWS, rows), :], F32)
        y = jnp.zeros((rows, D_MODEL), F32)
        for j in range(EXP_PER_GROUP):
            h1 = jnp.dot(xb, w1_ref[j], preferred_element_type=F32)
            h3 = jnp.dot(xb, w3_ref[j], preferred_element_type=F32)
            hb = (jax.nn.silu(h1) * h3).astype(BF16)
            y = y + jnp.dot(hb, w2_ref[j], preferred_element_type=F32) * gates[:, j:j + 1]
        for r in range(TILE_ROWS):
            y_ref[_tile_row(r, rows), :] = y[:, r * LANES:(r + 1) * LANES]

    @pl.when(i >= nused_ref[0])
    def _():
        y_ref[...] = jnp.zeros_like(y_ref)


def _expert_ffn(blk_grp, nused, xs, w1, w3, w2, layer):
    n_blk = xs.shape[0] // (MOE_BLOCK * TILE_ROWS)
    d = D_MODEL
    grp_w = lambda r, c: pl.BlockSpec((None, None, EXP_PER_GROUP, r, c),
                                      lambda i, bg, nu: (layer, bg[i], 0, 0, 0))
    tiles = pl.BlockSpec((MOE_BLOCK * TILE_ROWS, LANES), lambda i, bg, nu: (i, 0))
    grid_spec = pltpu.PrefetchScalarGridSpec(
        num_scalar_prefetch=2,
        grid=(n_blk,),
        in_specs=[tiles, grp_w(d, D_EXPERT), grp_w(d, D_EXPERT), grp_w(D_EXPERT, d)],
        out_specs=tiles,
    )
    return pl.pallas_call(
        _expert_body,
        grid_spec=grid_spec,
        out_shape=jax.ShapeDtypeStruct(xs.shape, F32),
        compiler_params=_cparams("arbitrary"),
        name="moe_experts",
    )(blk_grp, nused, xs, w1, w3, w2)


def _combine_body(dest_ref, dest_next_ref, x1_ref, g_ref, b_ref, ys_ref, o_ref, buf_ref, sems):
    i = pl.program_id(0)
    n_steps = pl.num_programs(0)
    tm = x1_ref.shape[0]

    def tile_copy(idx_ref, slot, r):
        return pltpu.make_async_copy(ys_ref.at[_tile(idx_ref[0, r])], buf_ref.at[slot, _tile(r)], sems.at[slot])

    def start_tiles(idx_ref, slot):
        def issue(r, _):
            tile_copy(idx_ref, slot, r).start()
            return 0
        lax.fori_loop(0, tm, issue, 0)

    @pl.when(i == 0)
    def _():
        start_tiles(dest_ref, 0)

    @pl.when(i + 1 < n_steps)
    def _():
        start_tiles(dest_next_ref, (i + 1) % 2)

    slot = i % 2

    def drain(r, _):
        tile_copy(dest_ref, slot, r).wait()
        return 0

    lax.fori_loop(0, tm, drain, 0)
    ffn = jnp.concatenate([buf_ref[slot, _tile_row(r, tm), :] for r in range(TILE_ROWS)], axis=1)
    o_ref[...] = _layer_norm(ALPHA * x1_ref[...] + ffn, g_ref[...], b_ref[...])


def _combine_ln(dest, x1, ln_g, ln_b, ys):
    t, d = x1.shape
    full = lambda a: pl.BlockSpec(a.shape, lambda i: (0,) * a.ndim)
    n_steps = t // TM_ROWS
    return pl.pallas_call(
        _combine_body,
        grid=(n_steps,),
        in_specs=[pl.BlockSpec((1, TM_ROWS), lambda i: (0, i), memory_space=pltpu.SMEM),
                  pl.BlockSpec((1, TM_ROWS), lambda i: (0, jnp.minimum(i + 1, n_steps - 1)),
                               memory_space=pltpu.SMEM),
                  pl.BlockSpec((TM_ROWS, d), lambda i: (i, 0)), full(ln_g), full(ln_b),
                  pl.BlockSpec(memory_space=pl.ANY)],
        out_specs=pl.BlockSpec((TM_ROWS, d), lambda i: (i, 0)),
        out_shape=jax.ShapeDtypeStruct((t, d), F32),
        scratch_shapes=[pltpu.VMEM((2, TM_ROWS * TILE_ROWS, LANES), F32), pltpu.SemaphoreType.DMA((2,))],
        compiler_params=_cparams("arbitrary"),
        name="moe_combine_ln",
    )(dest, dest, x1, ln_g, ln_b, ys)


def _moe_layer(x1, xp, grp, counts, w1, w3, w2, layer, ln_g, ln_b):
    t = x1.shape[0]
    n_rows = t + N_GROUPS * MOE_BLOCK
    n_blk = n_rows // MOE_BLOCK
    cnt = counts[:, 0].astype(jnp.int32)
    padded = (cnt + MOE_BLOCK - 1) // MOE_BLOCK * MOE_BLOCK
    pend = jnp.cumsum(padded)
    pstart = (pend - padded).astype(F32).reshape(N_GROUPS, 1)
    blk_row0 = jnp.arange(n_blk, dtype=jnp.int32) * MOE_BLOCK
    blk_grp = jnp.minimum(jnp.sum((pend[None, :] <= blk_row0[:, None]).astype(jnp.int32), axis=1), N_GROUPS - 1)
    nused = (pend[-1:] // MOE_BLOCK).astype(jnp.int32)
    dest = _dest_rows(grp, pstart)
    xs = _scatter_rows(dest, xp, n_rows)
    ys = _expert_ffn(blk_grp, nused, xs, w1, w3, w2, layer)
    return _combine_ln(dest, x1, ln_g, ln_b, ys)


def _rope_tables(seq):
    def tab(dim):
        inv = ROPE_THETA ** (-jnp.arange(0, dim, 2, dtype=F32) / dim)
        ang = jnp.arange(seq, dtype=F32)[:, None] * inv[None, :]
        return jnp.cos(ang), jnp.sin(ang)

    c, s = tab(HEAD_DIM)
    cos64 = jnp.tile(jnp.concatenate([c, c], axis=1), (1, LANES // HEAD_DIM))
    sin64 = jnp.tile(jnp.concatenate([-s, s], axis=1), (1, LANES // HEAD_DIM))
    c, s = tab(MLA_ROPE)
    pad_l, pad_r = MLA_ROPE_LANE, LANES - MLA_ROPE_LANE - MLA_ROPE
    cosm = jnp.concatenate([jnp.ones((seq, pad_l), F32), c, c, jnp.ones((seq, pad_r), F32)], axis=1)
    sinm = jnp.concatenate([jnp.zeros((seq, pad_l), F32), -s, s, jnp.zeros((seq, pad_r), F32)], axis=1)
    return cos64, sin64, cosm, sinm


def _prep_mixer_weights(w_in, q_norm, w_uq, kv_norm, w_ukv):
    d = w_in.shape[0]
    lat = MLA_Q_LORA + MLA_KV_LORA
    w_in_p = jnp.concatenate([
        w_in[:, :lat], jnp.zeros((d, MLA_ROPE_LANE), F32), w_in[:, lat:lat + MLA_ROPE],
        jnp.zeros((d, LANES - MLA_ROPE_LANE - MLA_ROPE), F32), w_in[:, lat + MLA_ROPE:]], axis=1).astype(BF16)
    wq = w_uq.reshape(MLA_Q_LORA, MLA_HEADS, MLA_NOPE + MLA_ROPE)
    wuq_p = jnp.pad(wq, ((0, 0), (0, 0), (0, MLA_HEAD_PAD - MLA_NOPE - MLA_ROPE)))
    wuq_p = wuq_p.reshape(MLA_Q_LORA, MLA_HEADS * MLA_HEAD_PAD).astype(BF16)
    wkv = w_ukv.reshape(MLA_KV_LORA, MLA_HEADS, MLA_NOPE + MLA_V)
    wk = jnp.pad(wkv[:, :, :MLA_NOPE], ((0, 0), (0, 0), (0, MLA_HEAD_PAD - MLA_NOPE)))
    wukv_p = jnp.concatenate([wk.reshape(MLA_KV_LORA, MLA_HEADS * MLA_HEAD_PAD),
                              wkv[:, :, MLA_NOPE:].reshape(MLA_KV_LORA, MLA_W)], axis=1).astype(BF16)
    return w_in_p, q_norm.reshape(1, -1), wuq_p, kv_norm.reshape(1, -1), wukv_p


def _mixer_heads(x2, w_in, q_norm, w_uq, kv_norm, w_ukv, tabs, batch, seq):
    w_in_p, qn, wuq_p, kvn, wukv_p = _prep_mixer_weights(w_in, q_norm, w_uq, kv_norm, w_ukv)
    qm, km, vm, qb, kb, vb, qc, kc, vc = _inproj(x2, w_in_p, qn, kvn, wuq_p, wukv_p, tabs, seq)
    o_a = _flash_attention(qm, km, vm, MLA_HEADS, batch, seq, "mla_attn")
    o_b = _flash_attention(qb, kb, vb, MOBA_HEADS, batch, seq, "moba_attn")
    o_c = _dil_attention(qc, kc, vc, batch, seq)
    return o_a, o_b, o_c


def kernel(x, w_in, mla_q_norm, mla_w_uq, mla_kv_norm, mla_w_ukv, w_out, ln1_g, ln1_b, router_w, router_b, moe_w1, moe_w3, moe_w2, ln2_g, ln2_b):
    batch, seq, d = x.shape
    tabs = _rope_tables(seq)
    x2 = x.reshape(batch * seq, d)
    rwt = router_w.T
    rb = router_b.reshape(N_EXPERTS, 1)
    by_group = lambda w: w.astype(BF16).reshape(DEPTH, N_GROUPS, EXP_PER_GROUP, *w.shape[2:])
    w1g, w3g, w2g = by_group(moe_w1), by_group(moe_w3), by_group(moe_w2)
    for l in range(DEPTH):
        o_a, o_b, o_c = _mixer_heads(x2, w_in[l], mla_q_norm[l], mla_w_uq[l], mla_kv_norm[l], mla_w_ukv[l],
                                     tabs, batch, seq)
        x1, xp, grp, counts = _outproj_ln_router(o_a, o_b, o_c, w_out[l].astype(BF16), x2,
                                                 ln1_g[l].reshape(1, d), ln1_b[l].reshape(1, d), rwt, rb)
        x2 = _moe_layer(x1, xp, grp, counts, w1g, w3g, w2g, l,
                        ln2_g[l].reshape(1, d), ln2_b[l].reshape(1, d))
    return x2.reshape(batch, seq, d)
```

```python
import functools

import jax
import jax.numpy as jnp
import numpy as np
from jax import lax
from jax.experimental import pallas as pl
from jax.experimental.pallas import tpu as pltpu

F32 = jnp.float32
BF16 = jnp.bfloat16
LANES = 128
NEG_INF = float("-inf")
VMEM_LIMIT_BYTES = 56 * 1024 * 1024

D_MODEL = 1024
DEPTH = 2
HEAD_DIM = 64
ROPE_THETA = 10000.0
MLA_HEADS = 4
MLA_NOPE = 64
MLA_ROPE = 32
MLA_V = 64
MLA_Q_LORA = 256
MLA_KV_LORA = 128
MOBA_HEADS = 6
MOBA_BLOCK = 256
MOBA_TOPK = 3
DIL_HEADS = 6
DIL_PAIRS = ((128, 1), (512, 4), (2048, 16))
MLA_W = MLA_HEADS * MLA_V
MOBA_W = MOBA_HEADS * HEAD_DIM
DIL_W = DIL_HEADS * HEAD_DIM
N_EXPERTS = 64
N_GROUPS = 8
EXP_PER_GROUP = N_EXPERTS // N_GROUPS
TOP_K = 2
D_EXPERT = 256
MOE_BLOCK = 512
ALPHA = (2 * DEPTH) ** 0.25
LN_EPS = 1e-5
RMS_EPS = 1e-6

SEG_A = 512
SEG_B = 3 * MOBA_W
SEG_C = 3 * DIL_W
N_IN_PAD = SEG_A + SEG_B + SEG_C
MLA_HEAD_PAD = 128
MLA_ROPE_LANE = 64

TILE_ROWS = 8
PACK_ROWS = D_MODEL // 2 // 128
TM_PROJ = 512
TQ = 256
KB_WIDE = 4
DIL_W_UNITS = 128
LOG2_E = 1.4426950408889634
ONES_ROW = HEAD_DIM
SEL_LANE = HEAD_DIM
MASK_BIG = 2.0 ** 100


def _cparams(*sem):
    return pltpu.CompilerParams(dimension_semantics=sem, vmem_limit_bytes=VMEM_LIMIT_BYTES)


def _rope(x, cos, sin_signed, half):
    lane = lax.broadcasted_iota(jnp.int32, x.shape, 1)
    first = ((lane // half) % 2) == 0
    rot = jnp.where(first, pltpu.roll(x, LANES - half, 1), pltpu.roll(x, half, 1))
    return x * cos + rot * sin_signed


def _split_bf16(x):
    hi = x.astype(BF16)
    return hi, (x - hi.astype(F32)).astype(BF16)


def _dot_nt_split(a_parts, b_parts):
    dn = (((1,), (1,)), ((), ()))
    dot = lambda x, y: lax.dot_general(x, y, dn, preferred_element_type=F32)
    (a_hi, a_lo), (b_hi, b_lo) = a_parts, b_parts
    return dot(a_hi, b_hi) + (dot(a_hi, b_lo) + dot(a_lo, b_hi))


def _rms(x, g):
    return x * lax.rsqrt(jnp.mean(jnp.square(x), axis=-1, keepdims=True) + RMS_EPS) * g


def _inproj_body(seq_blocks, x_ref, w_ref, qn_ref, kvn_ref, wuq_ref, wukv_ref,
                 cos64_ref, sin64_ref, cosm_ref, sinm_ref,
                 qm_ref, km_ref, vm_ref, qb_ref, kb_ref, vb_ref, qc_ref, kc_ref, vc_ref,
                 kmean_ref):
    i = pl.program_id(0)
    tm = x_ref.shape[0]
    n_kb = kmean_ref.shape[0]
    sblk = i % seq_blocks

    @pl.when(sblk == 0)
    def _():
        kmean_ref[...] = jnp.zeros_like(kmean_ref)

    xb = x_ref[...].astype(BF16)
    cos64, sin64 = cos64_ref[...], sin64_ref[...]
    cosm, sinm = cosm_ref[...], sinm_ref[...]

    h_a = jnp.dot(xb, w_ref[:, 0:SEG_A], preferred_element_type=F32)
    cq = _rms(h_a[:, 0:MLA_Q_LORA], qn_ref[...]).astype(BF16)
    ckv = _rms(h_a[:, MLA_Q_LORA:MLA_Q_LORA + MLA_KV_LORA], kvn_ref[...]).astype(BF16)
    kr = _rope(h_a[:, MLA_Q_LORA + MLA_KV_LORA:SEG_A], cosm, sinm, MLA_ROPE // 2)
    q_m = jnp.dot(cq, wuq_ref[...], preferred_element_type=F32)
    kv_m = jnp.dot(ckv, wukv_ref[...], preferred_element_type=F32)
    mla_scale = (MLA_NOPE + MLA_ROPE) ** -0.5 * LOG2_E
    lane_t = lax.broadcasted_iota(jnp.int32, (tm, LANES), 1)
    row_t = lax.broadcasted_iota(jnp.int32, (LANES, tm), 0)
    zeros_half = jnp.zeros((LANES - HEAD_DIM, tm), F32)

    def put_t(ref, h, val_t):
        for b in range(tm // TQ):
            ref[b, h * LANES:(h + 1) * LANES, :] = val_t[:, b * TQ:(b + 1) * TQ].astype(ref.dtype)

    def v_heads_t(pair):
        pt = pair.T
        return [jnp.where(row_t == ONES_ROW, 1.0, jnp.concatenate([pt[s * HEAD_DIM:(s + 1) * HEAD_DIM], zeros_half], 0))
                for s in range(2)]

    for h in range(MLA_HEADS):
        sl = slice(h * MLA_HEAD_PAD, (h + 1) * MLA_HEAD_PAD)
        put_t(qm_ref, h, (_rope(q_m[:, sl], cosm, sinm, MLA_ROPE // 2) * mla_scale).T)
        km_ref[:, sl] = (kv_m[:, sl] + kr).astype(BF16)
    for c in range(MLA_HEADS // 2):
        v_pair = kv_m[:, (MLA_HEADS + c) * LANES:(MLA_HEADS + c + 1) * LANES]
        for s, v_t in enumerate(v_heads_t(v_pair)):
            put_t(vm_ref, 2 * c + s, v_t)

    h_b = jnp.dot(xb, w_ref[:, SEG_A:SEG_A + SEG_B], preferred_element_type=F32)
    scale = HEAD_DIM ** -0.5
    lane = lax.broadcasted_iota(jnp.int32, (n_kb, LANES), 1)
    pos = sblk * tm + lax.broadcasted_iota(jnp.int32, (1, tm), 1)
    q_blk = pos // MOBA_BLOCK
    jio = lax.broadcasted_iota(jnp.int32, (n_kb, tm), 0)
    elig = jio < q_blk
    row_blk = (sblk * tm + lax.broadcasted_iota(jnp.int32, (tm, LANES), 0)) // MOBA_BLOCK
    blk_lane = lane_t == SEL_LANE + row_blk
    first_head = lane_t < HEAD_DIM

    def split(pair):
        return (jnp.where(first_head, pair, 0.0), jnp.where(first_head, pltpu.roll(pair, HEAD_DIM, 1), 0.0))

    for c in range(MOBA_W // LANES):
        sl = slice(c * LANES, (c + 1) * LANES)
        q = _rope(h_b[:, sl], cos64, sin64, HEAD_DIM // 2)
        k = _rope(h_b[:, MOBA_W + c * LANES:MOBA_W + (c + 1) * LANES], cos64, sin64, HEAD_DIM // 2)
        km = kmean_ref[:, sl]
        kb_row = lax.broadcasted_iota(jnp.int32, (n_kb, LANES), 0)
        for b in range(tm // MOBA_BLOCK):
            mean_b = jnp.mean(k[b * MOBA_BLOCK:(b + 1) * MOBA_BLOCK], axis=0, keepdims=True)
            km = jnp.where(kb_row == sblk * (tm // MOBA_BLOCK) + b, mean_b, km)
        kmean_ref[:, sl] = km
        q_t = (q * (scale * LOG2_E)).T
        k_heads = split(k)
        q_parts = _split_bf16(q)
        v_t = v_heads_t(h_b[:, 2 * MOBA_W + c * LANES:2 * MOBA_W + (c + 1) * LANES])
        for half in range(2):
            hs = slice((2 * c + half) * LANES, (2 * c + half + 1) * LANES)
            in_head = (lane // HEAD_DIM) == half
            gate = _dot_nt_split(_split_bf16(jnp.where(in_head, km, 0.0)), q_parts)
            gate = jnp.where(elig, gate, NEG_INF)
            q_rows = [q_t[half * HEAD_DIM:(half + 1) * HEAD_DIM]]
            for j in range(n_kb):
                gj = gate[j:j + 1, :]
                beats = (gate > gj) | ((gate == gj) & (jio < j))
                cnt = jnp.sum(beats.astype(F32), axis=0, keepdims=True)
                keep = ((cnt < MOBA_TOPK) & elig[j:j + 1, :]) | (q_blk == j)
                q_rows.append(jnp.where(keep, 0.0, -MASK_BIG))
            q_rows.append(jnp.zeros((LANES - SEL_LANE - n_kb, tm), F32))
            put_t(qb_ref, 2 * c + half, jnp.concatenate(q_rows, axis=0))
            kb_ref[:, hs] = jnp.where(blk_lane, 1.0, k_heads[half]).astype(BF16)
            put_t(vb_ref, 2 * c + half, v_t[half])

    h_c = jnp.dot(xb, w_ref[:, SEG_A + SEG_B:], preferred_element_type=F32)
    for c in range(DIL_W // LANES):
        sl = slice(c * LANES, (c + 1) * LANES)
        qc_ref[:, sl] = _rope(h_c[:, sl], cos64, sin64, HEAD_DIM // 2) * (scale * LOG2_E)
        kc_ref[:, sl] = _rope(h_c[:, DIL_W + c * LANES:DIL_W + (c + 1) * LANES], cos64, sin64, HEAD_DIM // 2)
    vc_ref[...] = h_c[:, 2 * DIL_W:]


def _inproj(x2, w_in_p, qn, kvn, wuq_p, wukv_p, tabs, seq):
    t = x2.shape[0]
    tm = TM_PROJ
    seq_blocks = seq // tm
    n_kb = seq // MOBA_BLOCK
    row = lambda w: pl.BlockSpec((tm, w), lambda i: (i, 0))
    full = lambda a: pl.BlockSpec(a.shape, lambda i: (0,) * a.ndim)
    tab = pl.BlockSpec((tm, LANES), lambda i: (i % seq_blocks, 0))
    out_shapes = [
        jax.ShapeDtypeStruct((t // TQ, MLA_HEADS * LANES, TQ), BF16),
        jax.ShapeDtypeStruct((t, MLA_HEADS * MLA_HEAD_PAD), BF16),
        jax.ShapeDtypeStruct((t // TQ, MLA_HEADS * LANES, TQ), BF16),
        jax.ShapeDtypeStruct((t // TQ, MOBA_HEADS * LANES, TQ), BF16),
        jax.ShapeDtypeStruct((t, MOBA_HEADS * LANES), BF16),
        jax.ShapeDtypeStruct((t // TQ, MOBA_HEADS * LANES, TQ), BF16),
        jax.ShapeDtypeStruct((t, DIL_W), F32),
        jax.ShapeDtypeStruct((t, DIL_W), F32),
        jax.ShapeDtypeStruct((t, DIL_W), F32),
    ]
    return pl.pallas_call(
        functools.partial(_inproj_body, seq_blocks),
        grid=(t // tm,),
        in_specs=[row(D_MODEL), full(w_in_p), full(qn), full(kvn), full(wuq_p), full(wukv_p),
                  tab, tab, tab, tab],
        out_specs=[row(s.shape[1]) if len(s.shape) == 2 else
                   pl.BlockSpec((tm // TQ,) + s.shape[1:], lambda i: (i, 0, 0)) for s in out_shapes],
        out_shape=out_shapes,
        scratch_shapes=[pltpu.VMEM((n_kb, MOBA_W), F32)],
        compiler_params=_cparams("arbitrary"),
        name="inproj",
    )(x2, w_in_p, qn, kvn, wuq_p, wukv_p, *tabs)


def _flash_body(n_heads, qt_ref, k_ref, vt_ref, o_ref, *state):
    i = pl.program_id(1)
    tq = qt_ref.shape[1]

    def head(h):
        return slice(h * LANES, (h + 1) * LANES)

    m_refs, acc_refs = state[:n_heads], state[n_heads:]
    key_le_query = (lax.broadcasted_iota(jnp.int32, (tq, tq), 0) <= lax.broadcasted_iota(jnp.int32, (tq, tq), 1))

    def scores(j, n, h):
        rows = pl.ds(pl.multiple_of(j * tq, tq), n * tq)
        return jnp.dot(k_ref[rows, head(h)], qt_ref[head(h), :], preferred_element_type=F32)

    def pv(j, n, h, p):
        return sum(jnp.dot(vt_ref[j + b, head(h), :], p[b * tq:(b + 1) * tq], preferred_element_type=F32)
                   for b in range(n))

    def own_block(h, s):
        s = jnp.where(key_le_query, s, NEG_INF)
        m = jnp.max(s, axis=0, keepdims=True)
        acc_refs[h][...] = pv(i, 1, h, jnp.exp2(s - m).astype(BF16))
        m_refs[h][...] = m

    def past_blocks(j, n, h, s):
        m_old = m_refs[h][...]
        m_new = jnp.maximum(m_old, jnp.max(s, axis=0, keepdims=True))
        p = jnp.exp2(s - m_new).astype(BF16)
        acc_refs[h][...] = jnp.exp2(m_old - m_new) * acc_refs[h][...] + pv(j, n, h, p)
        m_refs[h][...] = m_new

    def sweep(j, n, update):
        s_next = scores(j, n, 0)
        for h in range(n_heads):
            s = s_next
            if h + 1 < n_heads:
                s_next = scores(j, n, h + 1)
            update(h, s)

    sweep(i, 1, own_block)

    def wide_body(jj, _):
        sweep(jj * KB_WIDE, KB_WIDE, functools.partial(past_blocks, jj * KB_WIDE, KB_WIDE))
        return 0

    lax.fori_loop(0, i // KB_WIDE, wide_body, 0)
    done = (i // KB_WIDE) * KB_WIDE
    width = KB_WIDE // 2
    while width >= 1:
        take = ((i - done) // width) * width

        @pl.when(take > 0)
        def _(done=done, width=width):
            sweep(done, width, functools.partial(past_blocks, done, width))

        done = done + take
        width //= 2

    for c in range(n_heads // 2):
        halves = []
        for h in (2 * c, 2 * c + 1):
            acc = acc_refs[h][...]
            halves.append(acc[0:HEAD_DIM] / acc[ONES_ROW:ONES_ROW + 1])
        o_ref[:, head(c)] = jnp.concatenate(halves, axis=0).T.astype(o_ref.dtype)


def _flash_attention(qt, k, vt, n_heads, batch, seq, name):
    nq = seq // TQ
    width = n_heads * LANES
    return pl.pallas_call(
        functools.partial(_flash_body, n_heads),
        grid=(batch, nq),
        in_specs=[pl.BlockSpec((None, width, TQ), lambda b, i: (b * nq + i, 0, 0)),
                  pl.BlockSpec((seq, width), lambda b, i: (b, 0)),
                  pl.BlockSpec((nq, width, TQ), lambda b, i: (b, 0, 0))],
        out_specs=pl.BlockSpec((TQ, n_heads * HEAD_DIM), lambda b, i: (b * nq + i, 0)),
        out_shape=jax.ShapeDtypeStruct((batch * seq, n_heads * HEAD_DIM), BF16),
        scratch_shapes=[pltpu.VMEM((1, TQ), F32)] * n_heads + [pltpu.VMEM((LANES, TQ), F32)] * n_heads,
        compiler_params=_cparams("parallel", "arbitrary"),
        name=name,
    )(qt, k, vt)


def _dil_body(seq, q_ref, k_ref, v_ref, o_ref, ob_ref, lse_ref, ks_ref, vt_ref):
    w = DIL_W_UNITS
    n_all = seq // w
    frow = lax.broadcasted_iota(jnp.int32, (LANES, w), 0)
    key = lax.broadcasted_iota(jnp.int32, (2 * w, w), 0)
    qry = lax.broadcasted_iota(jnp.int32, (2 * w, w), 1)
    cur_ok = (key >= w) & (key - w <= qry)
    prev_ok = (key < w) & (key >= qry)
    zeros_half = jnp.zeros((LANES - HEAD_DIM, w), F32)
    ks_ref[0:w, :] = jnp.zeros((w, LANES), BF16)
    for h in range(2):
        vt_ref[h, 0] = jnp.zeros((LANES, w), BF16)
    for bi, (_, dil) in enumerate(DIL_PAIRS):
        n_blk = seq // (dil * w)

        def rows_of(idx, dil=dil, n_blk=n_blk):
            start = idx // n_blk + (idx % n_blk) * (w * dil)
            return pl.ds(start, w, stride=dil) if dil > 1 else pl.ds(pl.multiple_of(start, w), w)

        def stage(idx, _, rows_of=rows_of):
            rows = rows_of(idx)
            ks_ref[pl.ds(pl.multiple_of((idx + 1) * w, w), w), :] = k_ref[rows, :].astype(BF16)
            v_t = v_ref[rows, :].T
            for h in range(2):
                v_h = jnp.concatenate([v_t[h * HEAD_DIM:(h + 1) * HEAD_DIM], zeros_half], axis=0)
                vt_ref[h, idx + 1] = jnp.where(frow == ONES_ROW, 1.0, v_h).astype(BF16)
            return 0

        lax.fori_loop(0, n_all, stage, 0, unroll=4)

        def one(idx, _, bi=bi, n_blk=n_blk, rows_of=rows_of):
            rows = rows_of(idx)
            q_t = q_ref[rows, :].T
            keys = ks_ref[pl.ds(pl.multiple_of(idx * w, w), 2 * w), :]
            visible = cur_ok | (prev_ok & (idx % n_blk > 0))
            outs, lses = [], []
            for h in range(2):
                q_h = jnp.where((frow // HEAD_DIM) == h, q_t, 0.0).astype(BF16)
                s = jnp.where(visible, jnp.dot(keys, q_h, preferred_element_type=F32), NEG_INF)
                m = jnp.max(s, axis=0, keepdims=True)
                p = jnp.exp2(s - m).astype(BF16)
                acc = (jnp.dot(vt_ref[h, idx], p[0:w], preferred_element_type=F32)
                       + jnp.dot(vt_ref[h, idx + 1], p[w:2 * w], preferred_element_type=F32))
                l = acc[ONES_ROW:ONES_ROW + 1]
                outs.append(acc[0:HEAD_DIM] / l)
                lses.append(jnp.broadcast_to(m + jnp.log2(l), (HEAD_DIM, w)))
            ob_ref[bi, rows, :] = jnp.concatenate(outs, axis=0).T
            lse_ref[bi, rows, :] = jnp.concatenate(lses, axis=0).T
            return 0

        lax.fori_loop(0, n_all, one, 0, unroll=8)

    def merge(c, _):
        rows = pl.ds(pl.multiple_of(c * TQ, TQ), TQ)
        ls = [lse_ref[b, rows, :] for b in range(len(DIL_PAIRS))]
        top = functools.reduce(jnp.maximum, ls)
        ws = [jnp.exp2(x - top) for x in ls]
        num = sum(wb * ob_ref[b, rows, :] for b, wb in enumerate(ws))
        o_ref[rows, :] = (num / sum(ws)).astype(o_ref.dtype)
        return 0

    lax.fori_loop(0, seq // TQ, merge, 0)


def _dil_attention(qc, kc, vc, batch, seq):
    blk = pl.BlockSpec((seq, LANES), lambda b, g: (b, g))
    nbr = len(DIL_PAIRS)
    return pl.pallas_call(
        functools.partial(_dil_body, seq),
        grid=(batch, DIL_HEADS // 2),
        in_specs=[blk, blk, blk],
        out_specs=blk,
        out_shape=jax.ShapeDtypeStruct((batch * seq, DIL_W), BF16),
        scratch_shapes=[pltpu.VMEM((nbr, seq, LANES), F32), pltpu.VMEM((nbr, seq, LANES), F32),
                        pltpu.VMEM((seq + DIL_W_UNITS, LANES), BF16),
                        pltpu.VMEM((2, seq // DIL_W_UNITS + 1, LANES, DIL_W_UNITS), BF16)],
        compiler_params=_cparams("parallel", "parallel"),
        name="dil_attn",
    )(qc, kc, vc)


def _layer_norm(y, g, b):
    mu = jnp.mean(y, axis=-1, keepdims=True)
    yc = y - mu
    var = jnp.mean(jnp.square(yc), axis=-1, keepdims=True)
    return yc * lax.rsqrt(var + LN_EPS) * g + b


def _first_argmax(rows):
    best, idx = rows[0], jnp.zeros(rows[0].shape, jnp.int32)
    for j in range(1, len(rows)):
        upd = rows[j] > best
        idx = jnp.where(upd, j, idx)
        best = jnp.where(upd, rows[j], best)
    return best, idx


def _pick(rows, idx):
    out = rows[0]
    for j in range(1, len(rows)):
        out = jnp.where(idx == j, rows[j], out)
    return out


def _top2(rows):
    v1, i1 = _first_argmax(rows)
    rest = [jnp.where(i1 == j, NEG_INF, r) for j, r in enumerate(rows)]
    v2, i2 = _first_argmax(rest)
    return v1, i1, v2, i2


def _outproj_body(oa_ref, ob_ref, oc_ref, wo_ref, x_ref, g_ref, b_ref, rwt_ref, rb_ref,
                  x1_ref, xp_ref, grp_ref, cnt_ref):
    i = pl.program_id(0)
    tm = x_ref.shape[0]

    @pl.when(i == 0)
    def _():
        cnt_ref[...] = jnp.zeros_like(cnt_ref)

    mix = (jnp.dot(oa_ref[...], wo_ref[0:MLA_W, :], preferred_element_type=F32)
           + jnp.dot(ob_ref[...], wo_ref[MLA_W:MLA_W + MOBA_W, :], preferred_element_type=F32)
           + jnp.dot(oc_ref[...], wo_ref[MLA_W + MOBA_W:, :], preferred_element_type=F32))
    x1 = _layer_norm(ALPHA * x_ref[...] + mix, g_ref[...], b_ref[...])
    x1_ref[...] = x1
    bits = lax.bitcast_convert_type(x1.astype(BF16).astype(F32), jnp.uint32)
    words = bits[:, 0:D_MODEL // 2] | (bits[:, D_MODEL // 2:] >> 16)
    for r in range(PACK_ROWS):
        xp_ref[pl.ds(r, tm, stride=TILE_ROWS), :] = words[:, r * LANES:(r + 1) * LANES]
    for r in range(PACK_ROWS + 1, TILE_ROWS):
        xp_ref[pl.ds(r, tm, stride=TILE_ROWS), :] = jnp.zeros((tm, LANES), jnp.uint32)

    logits = _dot_nt_split(_split_bf16(rwt_ref[...]), _split_bf16(x1))
    s = jax.nn.sigmoid(logits)
    sb = s + rb_ref[...]
    s_rows = [s[e:e + 1, :] for e in range(N_EXPERTS)]
    sb_rows = [sb[e:e + 1, :] for e in range(N_EXPERTS)]
    grp = lambda rows, g: rows[g * EXP_PER_GROUP:(g + 1) * EXP_PER_GROUP]
    g_scores = []
    for g in range(N_GROUPS):
        v1, _, v2, _ = _top2(grp(sb_rows, g))
        g_scores.append(v1 + v2)
    _, g_sel = _first_argmax(g_scores)
    in_b = [_pick([grp(sb_rows, g)[j] for g in range(N_GROUPS)], g_sel) for j in range(EXP_PER_GROUP)]
    in_s = [_pick([grp(s_rows, g)[j] for g in range(N_GROUPS)], g_sel) for j in range(EXP_PER_GROUP)]
    _, j1, _, j2 = _top2(in_b)
    s1, s2 = _pick(in_s, j1), _pick(in_s, j2)
    den = s1 + s2
    g1, g2 = s1 / den, s2 / den
    gate_rows = [jnp.where(j1 == j, g1, 0.0) + jnp.where(j2 == j, g2, 0.0) for j in range(EXP_PER_GROUP)]
    gate_rows.append(jnp.zeros((LANES - EXP_PER_GROUP, tm), F32))
    xp_ref[pl.ds(PACK_ROWS, tm, stride=TILE_ROWS), :] = lax.bitcast_convert_type(
        jnp.concatenate(gate_rows, axis=0).T, jnp.uint32)
    grp_ref[...] = jnp.concatenate([g_sel.astype(F32), jnp.zeros((7, tm), F32)], axis=0)
    gio = lax.broadcasted_iota(jnp.int32, (N_GROUPS, tm), 0)
    cnt_ref[...] += jnp.sum(jnp.where(gio == g_sel, 1.0, 0.0), axis=1, keepdims=True)


def _outproj_ln_router(o_a, o_b, o_c, w_out_b, x2, ln_g, ln_b, rwt, rb):
    t = x2.shape[0]
    tm = TM_PROJ
    row = lambda w: pl.BlockSpec((tm, w), lambda i: (i, 0))
    full = lambda a: pl.BlockSpec(a.shape, lambda i: (0,) * a.ndim)
    return pl.pallas_call(
        _outproj_body,
        grid=(t // tm,),
        in_specs=[row(MLA_W), row(MOBA_W), row(DIL_W), full(w_out_b), row(D_MODEL), full(ln_g), full(ln_b),
                  full(rwt), full(rb)],
        out_specs=[row(D_MODEL), pl.BlockSpec((tm * TILE_ROWS, LANES), lambda i: (i, 0)),
                   pl.BlockSpec((8, tm), lambda i: (0, i)),
                   pl.BlockSpec((N_GROUPS, LANES), lambda i: (0, 0))],
        out_shape=[jax.ShapeDtypeStruct((t, D_MODEL), F32),
                   jax.ShapeDtypeStruct((t * TILE_ROWS, LANES), jnp.uint32),
                   jax.ShapeDtypeStruct((8, t), F32),
                   jax.ShapeDtypeStruct((N_GROUPS, LANES), F32)],
        compiler_params=_cparams("arbitrary"),
        name="outproj_ln_router",
    )(o_a, o_b, o_c, w_out_b, x2, ln_g, ln_b, rwt, rb)


TN_SORT = 512
TM_ROWS = 1024
DMA_UNROLL = 8


def _dest_body(grp_ref, pstart_ref, dest_ref, carry_ref):
    n = grp_ref.shape[1]

    @pl.when(pl.program_id(0) == 0)
    def _():
        carry_ref[...] = jnp.broadcast_to(pstart_ref[...], carry_ref.shape)

    before = (lax.broadcasted_iota(jnp.int32, (n, n), 0) < lax.broadcasted_iota(jnp.int32, (n, n), 1))
    before = jnp.where(before, 1.0, 0.0).astype(BF16)
    gio = lax.broadcasted_iota(jnp.int32, (N_GROUPS, n), 0)
    hit = jnp.where(gio == grp_ref[0:1, :].astype(jnp.int32), 1.0, 0.0)
    earlier = jnp.dot(hit.astype(BF16), before, preferred_element_type=F32)
    dest = jnp.sum(hit * (earlier + carry_ref[:, 0:1]), axis=0, keepdims=True)
    carry_ref[...] += jnp.sum(hit, axis=1, keepdims=True)
    dest_ref[...] = dest.astype(jnp.int32)


def _dest_rows(grp, pstart):
    t = grp.shape[1]
    return pl.pallas_call(
        _dest_body,
        grid=(t // TN_SORT,),
        in_specs=[pl.BlockSpec((8, TN_SORT), lambda i: (0, i)),
                  pl.BlockSpec((N_GROUPS, 1), lambda i: (0, 0))],
        out_specs=pl.BlockSpec((1, TN_SORT), lambda i: (0, i)),
        out_shape=jax.ShapeDtypeStruct((1, t), jnp.int32),
        scratch_shapes=[pltpu.VMEM((N_GROUPS, LANES), F32)],
        compiler_params=_cparams("arbitrary"),
        name="moe_dest",
    )(grp, pstart)


def _tile(row):
    return pl.ds(pl.multiple_of(row * TILE_ROWS, TILE_ROWS), TILE_ROWS)


def _tile_row(r, n):
    return pl.ds(r, n, stride=TILE_ROWS)


def _scatter_body(dest_ref, x_ref, xs_in_ref, xs_ref, sem):
    del xs_in_ref
    tm = x_ref.shape[0] // TILE_ROWS

    def tile_copy(r):
        return pltpu.make_async_copy(x_ref.at[_tile(r)], xs_ref.at[_tile(dest_ref[0, r])], sem)

    def issue(r, _):
        tile_copy(r).start()
        return 0

    def drain(r, _):
        tile_copy(r).wait()
        return 0

    lax.fori_loop(0, tm, issue, 0, unroll=DMA_UNROLL)
    lax.fori_loop(0, tm, drain, 0, unroll=DMA_UNROLL)


def _scatter_rows(dest, xp, n_rows):
    t = xp.shape[0] // TILE_ROWS
    xs_init = jnp.zeros((n_rows * TILE_ROWS, LANES), xp.dtype)
    return pl.pallas_call(
        _scatter_body,
        grid=(t // TM_ROWS,),
        in_specs=[pl.BlockSpec((1, TM_ROWS), lambda i: (0, i), memory_space=pltpu.SMEM),
                  pl.BlockSpec((TM_ROWS * TILE_ROWS, LANES), lambda i: (i, 0)),
                  pl.BlockSpec(memory_space=pl.ANY)],
        out_specs=pl.BlockSpec(memory_space=pl.ANY),
        out_shape=jax.ShapeDtypeStruct(xs_init.shape, xp.dtype),
        scratch_shapes=[pltpu.SemaphoreType.DMA(())],
        input_output_aliases={2: 0},
        compiler_params=_cparams("arbitrary"),
        name="moe_scatter",
    )(dest, xp, xs_init)


def _expert_body(blk_grp_ref, nused_ref, xs_ref, w1_ref, w3_ref, w2_ref, y_ref):
    del blk_grp_ref
    i = pl.program_id(0)
    rows = xs_ref.shape[0] // TILE_ROWS

    @pl.when(i < nused_ref[0])
    def _():
        words = jnp.concatenate([xs_ref[_tile_row(r, rows), :] for r in range(PACK_ROWS)], axis=1)
        unpack = lambda bits: lax.bitcast_convert_type(bits, F32).astype(BF16)
        xb = jnp.concatenate([unpack(words & jnp.uint32(0xFFFF0000)), unpack(words << 16)], axis=1)
        gates = lax.bitcast_convert_type(xs_ref[_tile_row(PACK_ROWS, rows), :], F32)
        y = jnp.zeros((rows, D_MODEL), F32)
        for j in range(EXP_PER_GROUP):
            h1 = jnp.dot(xb, w1_ref[j], preferred_element_type=F32)
            h3 = jnp.dot(xb, w3_ref[j], preferred_element_type=F32)
            hb = (jax.nn.silu(h1) * h3).astype(BF16)
            y = y + jnp.dot(hb, w2_ref[j], preferred_element_type=F32) * gates[:, j:j + 1]
        for r in range(TILE_ROWS):
            y_ref[_tile_row(r, rows), :] = y[:, r * LANES:(r + 1) * LANES]

    @pl.when(i >= nused_ref[0])
    def _():
        y_ref[...] = jnp.zeros_like(y_ref)


def _expert_ffn(blk_grp, nused, xs, w1, w3, w2, layer):
    n_blk = xs.shape[0] // (MOE_BLOCK * TILE_ROWS)
    d = D_MODEL
    grp_w = lambda r, c: pl.BlockSpec((None, None, EXP_PER_GROUP, r, c),
                                      lambda i, bg, nu: (layer, bg[i], 0, 0, 0))
    tiles = pl.BlockSpec((MOE_BLOCK * TILE_ROWS, LANES), lambda i, bg, nu: (i, 0))
    grid_spec = pltpu.PrefetchScalarGridSpec(
        num_scalar_prefetch=2,
        grid=(n_blk,),
        in_specs=[tiles, grp_w(d, D_EXPERT), grp_w(d, D_EXPERT), grp_w(D_EXPERT, d)],
        out_specs=tiles,
    )
    return pl.pallas_call(
        _expert_body,
        grid_spec=grid_spec,
        out_shape=jax.ShapeDtypeStruct(xs.shape, F32),
        compiler_params=_cparams("arbitrary"),
        name="moe_experts",
    )(blk_grp, nused, xs, w1, w3, w2)


def _combine_body(dest_ref, dest_next_ref, x1_ref, g_ref, b_ref, ys_ref, o_ref, buf_ref, sems):
    i = pl.program_id(0)
    n_steps = pl.num_programs(0)
    tm = x1_ref.shape[0]

    def tile_copy(idx_ref, slot, r):
        return pltpu.make_async_copy(ys_ref.at[_tile(idx_ref[0, r])], buf_ref.at[slot, _tile(r)], sems.at[slot])

    def start_tiles(idx_ref, slot):
        def issue(r, _):
            tile_copy(idx_ref, slot, r).start()
            return 0
        lax.fori_loop(0, tm, issue, 0, unroll=DMA_UNROLL)

    @pl.when(i == 0)
    def _():
        start_tiles(dest_ref, 0)

    @pl.when(i + 1 < n_steps)
    def _():
        start_tiles(dest_next_ref, (i + 1) % 2)

    slot = i % 2

    def drain(r, _):
        tile_copy(dest_ref, slot, r).wait()
        return 0

    lax.fori_loop(0, tm, drain, 0, unroll=DMA_UNROLL)
    ffn = jnp.concatenate([buf_ref[slot, _tile_row(r, tm), :] for r in range(TILE_ROWS)], axis=1)
    o_ref[...] = _layer_norm(ALPHA * x1_ref[...] + ffn, g_ref[...], b_ref[...])


def _combine_ln(dest, x1, ln_g, ln_b, ys):
    t, d = x1.shape
    full = lambda a: pl.BlockSpec(a.shape, lambda i: (0,) * a.ndim)
    n_steps = t // TM_ROWS
    return pl.pallas_call(
        _combine_body,
        grid=(n_steps,),
        in_specs=[pl.BlockSpec((1, TM_ROWS), lambda i: (0, i), memory_space=pltpu.SMEM),
                  pl.BlockSpec((1, TM_ROWS), lambda i: (0, jnp.minimum(i + 1, n_steps - 1)),
                               memory_space=pltpu.SMEM),
                  pl.BlockSpec((TM_ROWS, d), lambda i: (i, 0)), full(ln_g), full(ln_b),
                  pl.BlockSpec(memory_space=pl.ANY)],
        out_specs=pl.BlockSpec((TM_ROWS, d), lambda i: (i, 0)),
        out_shape=jax.ShapeDtypeStruct((t, d), F32),
        scratch_shapes=[pltpu.VMEM((2, TM_ROWS * TILE_ROWS, LANES), F32), pltpu.SemaphoreType.DMA((2,))],
        compiler_params=_cparams("arbitrary"),
        name="moe_combine_ln",
    )(dest, dest, x1, ln_g, ln_b, ys)


def _moe_layer(x1, xp, grp, counts, w1, w3, w2, layer, ln_g, ln_b):
    t = x1.shape[0]
    n_rows = t + N_GROUPS * MOE_BLOCK
    n_blk = n_rows // MOE_BLOCK
    cnt = counts[:, 0].astype(jnp.int32)
    padded = (cnt + MOE_BLOCK - 1) // MOE_BLOCK * MOE_BLOCK
    pend = jnp.cumsum(padded)
    pstart = (pend - padded).astype(F32).reshape(N_GROUPS, 1)
    blk_row0 = jnp.arange(n_blk, dtype=jnp.int32) * MOE_BLOCK
    blk_grp = jnp.minimum(jnp.sum((pend[None, :] <= blk_row0[:, None]).astype(jnp.int32), axis=1), N_GROUPS - 1)
    nused = (pend[-1:] // MOE_BLOCK).astype(jnp.int32)
    dest = _dest_rows(grp, pstart)
    xs = _scatter_rows(dest, xp, n_rows)
    ys = _expert_ffn(blk_grp, nused, xs, w1, w3, w2, layer)
    return _combine_ln(dest, x1, ln_g, ln_b, ys)


def _rope_tables(seq):
    def tab(dim):
        inv = ROPE_THETA ** (-jnp.arange(0, dim, 2, dtype=F32) / dim)
        ang = jnp.arange(seq, dtype=F32)[:, None] * inv[None, :]
        return jnp.cos(ang), jnp.sin(ang)

    c, s = tab(HEAD_DIM)
    cos64 = jnp.tile(jnp.concatenate([c, c], axis=1), (1, LANES // HEAD_DIM))
    sin64 = jnp.tile(jnp.concatenate([-s, s], axis=1), (1, LANES // HEAD_DIM))
    c, s = tab(MLA_ROPE)
    pad_l, pad_r = MLA_ROPE_LANE, LANES - MLA_ROPE_LANE - MLA_ROPE
    cosm = jnp.concatenate([jnp.ones((seq, pad_l), F32), c, c, jnp.ones((seq, pad_r), F32)], axis=1)
    sinm = jnp.concatenate([jnp.zeros((seq, pad_l), F32), -s, s, jnp.zeros((seq, pad_r), F32)], axis=1)
    return cos64, sin64, cosm, sinm


def _prep_mixer_weights(w_in, q_norm, w_uq, kv_norm, w_ukv):
    d = w_in.shape[0]
    lat = MLA_Q_LORA + MLA_KV_LORA
    w_in_p = jnp.concatenate([
        w_in[:, :lat], jnp.zeros((d, MLA_ROPE_LANE), F32), w_in[:, lat:lat + MLA_ROPE],
        jnp.zeros((d, LANES - MLA_ROPE_LANE - MLA_ROPE), F32), w_in[:, lat + MLA_ROPE:]], axis=1).astype(BF16)
    wq = w_uq.reshape(MLA_Q_LORA, MLA_HEADS, MLA_NOPE + MLA_ROPE)
    wuq_p = jnp.pad(wq, ((0, 0), (0, 0), (0, MLA_HEAD_PAD - MLA_NOPE - MLA_ROPE)))
    wuq_p = wuq_p.reshape(MLA_Q_LORA, MLA_HEADS * MLA_HEAD_PAD).astype(BF16)
    wkv = w_ukv.reshape(MLA_KV_LORA, MLA_HEADS, MLA_NOPE + MLA_V)
    wk = jnp.pad(wkv[:, :, :MLA_NOPE], ((0, 0), (0, 0), (0, MLA_HEAD_PAD - MLA_NOPE)))
    wukv_p = jnp.concatenate([wk.reshape(MLA_KV_LORA, MLA_HEADS * MLA_HEAD_PAD),
                              wkv[:, :, MLA_NOPE:].reshape(MLA_KV_LORA, MLA_W)], axis=1).astype(BF16)
    return w_in_p, q_norm.reshape(1, -1), wuq_p, kv_norm.reshape(1, -1), wukv_p


def _mixer_heads(x2, w_in, q_norm, w_uq, kv_norm, w_ukv, tabs, batch, seq):
    w_in_p, qn, wuq_p, kvn, wukv_p = _prep_mixer_weights(w_in, q_norm, w_uq, kv_norm, w_ukv)
    qm, km, vm, qb, kb, vb, qc, kc, vc = _inproj(x2, w_in_p, qn, kvn, wuq_p, wukv_p, tabs, seq)
    o_a = _flash_attention(qm, km, vm, MLA_HEADS, batch, seq, "mla_attn")
    o_b = _flash_attention(qb, kb, vb, MOBA_HEADS, batch, seq, "moba_attn")
    o_c = _dil_attention(qc, kc, vc, batch, seq)
    return o_a, o_b, o_c


def kernel(x, w_in, mla_q_norm, mla_w_uq, mla_kv_norm, mla_w_ukv, w_out, ln1_g, ln1_b, router_w, router_b, moe_w1, moe_w3, moe_w2, ln2_g, ln2_b):
    batch, seq, d = x.shape
    tabs = _rope_tables(seq)
    x2 = x.reshape(batch * seq, d)
    rwt = router_w.T
    rb = router_b.reshape(N_EXPERTS, 1)
    by_group = lambda w: w.astype(BF16).reshape(DEPTH, N_GROUPS, EXP_PER_GROUP, *w.shape[2:])
    w1g, w3g, w2g = by_group(moe_w1), by_group(moe_w3), by_group(moe_w2)
    for l in range(DEPTH):
        o_a, o_b, o_c = _mixer_heads(x2, w_in[l], mla_q_norm[l], mla_w_uq[l], mla_kv_norm[l], mla_w_ukv[l],
                                     tabs, batch, seq)
        x1, xp, grp, counts = _outproj_ln_router(o_a, o_b, o_c, w_out[l].astype(BF16), x2,
                                                 ln1_g[l].reshape(1, d), ln1_b[l].reshape(1, d), rwt, rb)
        x2 = _moe_layer(x1, xp, grp, counts, w1g, w3g, w2g, l,
                        ln2_g[l].reshape(1, d), ln2_b[l].reshape(1, d))
    return x2.reshape(batch, seq, d)
```

```python
import functools

import jax
import jax.numpy as jnp
import numpy as np
from jax import lax
from jax.experimental import pallas as pl
from jax.experimental.pallas import tpu as pltpu

F32 = jnp.float32
BF16 = jnp.bfloat16
LANES = 128
NEG_INF = float("-inf")
VMEM_LIMIT_BYTES = 56 * 1024 * 1024

D_MODEL = 1024
DEPTH = 2
HEAD_DIM = 64
ROPE_THETA = 10000.0
MLA_HEADS = 4
MLA_NOPE = 64
MLA_ROPE = 32
MLA_V = 64
MLA_Q_LORA = 256
MLA_KV_LORA = 128
MOBA_HEADS = 6
MOBA_BLOCK = 256
MOBA_TOPK = 3
DIL_HEADS = 6
DIL_PAIRS = ((128, 1), (512, 4), (2048, 16))
MLA_W = MLA_HEADS * MLA_V
MOBA_W = MOBA_HEADS * HEAD_DIM
DIL_W = DIL_HEADS * HEAD_DIM
N_EXPERTS = 64
N_GROUPS = 8
EXP_PER_GROUP = N_EXPERTS // N_GROUPS
TOP_K = 2
D_EXPERT = 256
MOE_BLOCK = 512
ALPHA = (2 * DEPTH) ** 0.25
LN_EPS = 1e-5
RMS_EPS = 1e-6

SEG_A = 512
SEG_B = 3 * MOBA_W
SEG_C = 3 * DIL_W
N_IN_PAD = SEG_A + SEG_B + SEG_C
MLA_HEAD_PAD = 128
MLA_ROPE_LANE = 64

TILE_ROWS = 8
PACK_ROWS = D_MODEL // 2 // 128
TM_PROJ = 512
TQ = 256
SCORES_AHEAD = 6
KB_WIDE = 4
DIL_W_UNITS = 128
LOG2_E = 1.4426950408889634
ONES_ROW = HEAD_DIM
SEL_LANE = HEAD_DIM
MASK_BIG = 2.0 ** 100


def _cparams(*sem):
    return pltpu.CompilerParams(dimension_semantics=sem, vmem_limit_bytes=VMEM_LIMIT_BYTES)


def _rope(x, cos, sin_signed, half):
    lane = lax.broadcasted_iota(jnp.int32, x.shape, 1)
    first = ((lane // half) % 2) == 0
    rot = jnp.where(first, pltpu.roll(x, LANES - half, 1), pltpu.roll(x, half, 1))
    return x * cos + rot * sin_signed


def _split_bf16(x):
    hi = x.astype(BF16)
    return hi, (x - hi.astype(F32)).astype(BF16)


def _dot_nt_split(a_parts, b_parts):
    dn = (((1,), (1,)), ((), ()))
    dot = lambda x, y: lax.dot_general(x, y, dn, preferred_element_type=F32)
    (a_hi, a_lo), (b_hi, b_lo) = a_parts, b_parts
    return dot(a_hi, b_hi) + (dot(a_hi, b_lo) + dot(a_lo, b_hi))


def _rms(x, g):
    return x * lax.rsqrt(jnp.mean(jnp.square(x), axis=-1, keepdims=True) + RMS_EPS) * g


def _inproj_body(seq_blocks, x_ref, w_ref, qn_ref, kvn_ref, wuq_ref, wukv_ref,
                 cos64_ref, sin64_ref, cosm_ref, sinm_ref,
                 qm_ref, km_ref, vm_ref, qb_ref, kb_ref, vb_ref, qc_ref, kc_ref, vc_ref,
                 kmean_ref):
    i = pl.program_id(0)
    tm = x_ref.shape[0]
    n_kb = kmean_ref.shape[0]
    sblk = i % seq_blocks

    @pl.when(sblk == 0)
    def _():
        kmean_ref[...] = jnp.zeros_like(kmean_ref)

    xb = x_ref[...].astype(BF16)
    cos64, sin64 = cos64_ref[...], sin64_ref[...]
    cosm, sinm = cosm_ref[...], sinm_ref[...]

    h_a = jnp.dot(xb, w_ref[:, 0:SEG_A], preferred_element_type=F32)
    cq = _rms(h_a[:, 0:MLA_Q_LORA], qn_ref[...]).astype(BF16)
    ckv = _rms(h_a[:, MLA_Q_LORA:MLA_Q_LORA + MLA_KV_LORA], kvn_ref[...]).astype(BF16)
    kr = _rope(h_a[:, MLA_Q_LORA + MLA_KV_LORA:SEG_A], cosm, sinm, MLA_ROPE // 2)
    q_m = jnp.dot(cq, wuq_ref[...], preferred_element_type=F32)
    kv_m = jnp.dot(ckv, wukv_ref[...], preferred_element_type=F32)
    mla_scale = (MLA_NOPE + MLA_ROPE) ** -0.5 * LOG2_E
    lane_t = lax.broadcasted_iota(jnp.int32, (tm, LANES), 1)
    row_t = lax.broadcasted_iota(jnp.int32, (LANES, tm), 0)
    zeros_half = jnp.zeros((LANES - HEAD_DIM, tm), F32)

    def put_t(ref, h, val_t):
        for b in range(tm // TQ):
            ref[b, h * LANES:(h + 1) * LANES, :] = val_t[:, b * TQ:(b + 1) * TQ].astype(ref.dtype)

    def v_heads_t(pair):
        pt = pair.T
        return [jnp.where(row_t == ONES_ROW, 1.0, jnp.concatenate([pt[s * HEAD_DIM:(s + 1) * HEAD_DIM], zeros_half], 0))
                for s in range(2)]

    for h in range(MLA_HEADS):
        sl = slice(h * MLA_HEAD_PAD, (h + 1) * MLA_HEAD_PAD)
        put_t(qm_ref, h, (_rope(q_m[:, sl], cosm, sinm, MLA_ROPE // 2) * mla_scale).T)
        km_ref[:, sl] = (kv_m[:, sl] + kr).astype(BF16)
    for c in range(MLA_HEADS // 2):
        v_pair = kv_m[:, (MLA_HEADS + c) * LANES:(MLA_HEADS + c + 1) * LANES]
        for s, v_t in enumerate(v_heads_t(v_pair)):
            put_t(vm_ref, 2 * c + s, v_t)

    h_b = jnp.dot(xb, w_ref[:, SEG_A:SEG_A + SEG_B], preferred_element_type=F32)
    scale = HEAD_DIM ** -0.5
    lane = lax.broadcasted_iota(jnp.int32, (n_kb, LANES), 1)
    pos = sblk * tm + lax.broadcasted_iota(jnp.int32, (1, tm), 1)
    q_blk = pos // MOBA_BLOCK
    jio = lax.broadcasted_iota(jnp.int32, (n_kb, tm), 0)
    elig = jio < q_blk
    row_blk = (sblk * tm + lax.broadcasted_iota(jnp.int32, (tm, LANES), 0)) // MOBA_BLOCK
    blk_lane = lane_t == SEL_LANE + row_blk
    first_head = lane_t < HEAD_DIM

    def split(pair):
        return (jnp.where(first_head, pair, 0.0), jnp.where(first_head, pltpu.roll(pair, HEAD_DIM, 1), 0.0))

    for c in range(MOBA_W // LANES):
        sl = slice(c * LANES, (c + 1) * LANES)
        q = _rope(h_b[:, sl], cos64, sin64, HEAD_DIM // 2)
        k = _rope(h_b[:, MOBA_W + c * LANES:MOBA_W + (c + 1) * LANES], cos64, sin64, HEAD_DIM // 2)
        km = kmean_ref[:, sl]
        kb_row = lax.broadcasted_iota(jnp.int32, (n_kb, LANES), 0)
        for b in range(tm // MOBA_BLOCK):
            mean_b = jnp.mean(k[b * MOBA_BLOCK:(b + 1) * MOBA_BLOCK], axis=0, keepdims=True)
            km = jnp.where(kb_row == sblk * (tm // MOBA_BLOCK) + b, mean_b, km)
        kmean_ref[:, sl] = km
        q_t = (q * (scale * LOG2_E)).T
        k_heads = split(k)
        q_parts = _split_bf16(q)
        v_t = v_heads_t(h_b[:, 2 * MOBA_W + c * LANES:2 * MOBA_W + (c + 1) * LANES])
        for half in range(2):
            hs = slice((2 * c + half) * LANES, (2 * c + half + 1) * LANES)
            in_head = (lane // HEAD_DIM) == half
            gate = _dot_nt_split(_split_bf16(jnp.where(in_head, km, 0.0)), q_parts)
            gate = jnp.where(elig, gate, NEG_INF)
            q_rows = [q_t[half * HEAD_DIM:(half + 1) * HEAD_DIM]]
            for j in range(n_kb):
                gj = gate[j:j + 1, :]
                beats = (gate > gj) | ((gate == gj) & (jio < j))
                cnt = jnp.sum(beats.astype(F32), axis=0, keepdims=True)
                keep = ((cnt < MOBA_TOPK) & elig[j:j + 1, :]) | (q_blk == j)
                q_rows.append(jnp.where(keep, 0.0, -MASK_BIG))
            q_rows.append(jnp.zeros((LANES - SEL_LANE - n_kb, tm), F32))
            put_t(qb_ref, 2 * c + half, jnp.concatenate(q_rows, axis=0))
            kb_ref[:, hs] = jnp.where(blk_lane, 1.0, k_heads[half]).astype(BF16)
            put_t(vb_ref, 2 * c + half, v_t[half])

    h_c = jnp.dot(xb, w_ref[:, SEG_A + SEG_B:], preferred_element_type=F32)
    for c in range(DIL_W // LANES):
        sl = slice(c * LANES, (c + 1) * LANES)
        qc_ref[:, sl] = _rope(h_c[:, sl], cos64, sin64, HEAD_DIM // 2) * (scale * LOG2_E)
        kc_ref[:, sl] = _rope(h_c[:, DIL_W + c * LANES:DIL_W + (c + 1) * LANES], cos64, sin64, HEAD_DIM // 2)
    vc_ref[...] = h_c[:, 2 * DIL_W:]


def _inproj(x2, w_in_p, qn, kvn, wuq_p, wukv_p, tabs, seq):
    t = x2.shape[0]
    tm = TM_PROJ
    seq_blocks = seq // tm
    n_kb = seq // MOBA_BLOCK
    row = lambda w: pl.BlockSpec((tm, w), lambda i: (i, 0))
    full = lambda a: pl.BlockSpec(a.shape, lambda i: (0,) * a.ndim)
    tab = pl.BlockSpec((tm, LANES), lambda i: (i % seq_blocks, 0))
    out_shapes = [
        jax.ShapeDtypeStruct((t // TQ, MLA_HEADS * LANES, TQ), BF16),
        jax.ShapeDtypeStruct((t, MLA_HEADS * MLA_HEAD_PAD), BF16),
        jax.ShapeDtypeStruct((t // TQ, MLA_HEADS * LANES, TQ), BF16),
        jax.ShapeDtypeStruct((t // TQ, MOBA_HEADS * LANES, TQ), BF16),
        jax.ShapeDtypeStruct((t, MOBA_HEADS * LANES), BF16),
        jax.ShapeDtypeStruct((t // TQ, MOBA_HEADS * LANES, TQ), BF16),
        jax.ShapeDtypeStruct((t, DIL_W), F32),
        jax.ShapeDtypeStruct((t, DIL_W), F32),
        jax.ShapeDtypeStruct((t, DIL_W), F32),
    ]
    return pl.pallas_call(
        functools.partial(_inproj_body, seq_blocks),
        grid=(t // tm,),
        in_specs=[row(D_MODEL), full(w_in_p), full(qn), full(kvn), full(wuq_p), full(wukv_p),
                  tab, tab, tab, tab],
        out_specs=[row(s.shape[1]) if len(s.shape) == 2 else
                   pl.BlockSpec((tm // TQ,) + s.shape[1:], lambda i: (i, 0, 0)) for s in out_shapes],
        out_shape=out_shapes,
        scratch_shapes=[pltpu.VMEM((n_kb, MOBA_W), F32)],
        compiler_params=_cparams("arbitrary"),
        name="inproj",
    )(x2, w_in_p, qn, kvn, wuq_p, wukv_p, *tabs)


def _flash_body(n_heads, qt_ref, k_ref, vt_ref, o_ref, *state):
    i = pl.program_id(1)
    tq = qt_ref.shape[1]

    def head(h):
        return slice(h * LANES, (h + 1) * LANES)

    m_refs, acc_refs = state[:n_heads], state[n_heads:]
    key_le_query = (lax.broadcasted_iota(jnp.int32, (tq, tq), 0) <= lax.broadcasted_iota(jnp.int32, (tq, tq), 1))

    def scores(j, n, h):
        rows = pl.ds(pl.multiple_of(j * tq, tq), n * tq)
        return jnp.dot(k_ref[rows, head(h)], qt_ref[head(h), :], preferred_element_type=F32)

    def pv(j, n, h, p):
        return sum(jnp.dot(vt_ref[j + b, head(h), :], p[b * tq:(b + 1) * tq], preferred_element_type=F32)
                   for b in range(n))

    def own_block(h, s):
        s = jnp.where(key_le_query, s, NEG_INF)
        m = jnp.max(s, axis=0, keepdims=True)
        acc_refs[h][...] = pv(i, 1, h, jnp.exp2(s - m).astype(BF16))
        m_refs[h][...] = m

    def past_blocks(j, n, h, s):
        m_old = m_refs[h][...]
        m_new = jnp.maximum(m_old, jnp.max(s, axis=0, keepdims=True))
        p = jnp.exp2(s - m_new).astype(BF16)
        acc_refs[h][...] = jnp.exp2(m_old - m_new) * acc_refs[h][...] + pv(j, n, h, p)
        m_refs[h][...] = m_new

    def sweep(j, n, update):
        ahead = [scores(j, n, h) for h in range(min(SCORES_AHEAD, n_heads))]
        for h in range(n_heads):
            if h + SCORES_AHEAD < n_heads:
                ahead.append(scores(j, n, h + SCORES_AHEAD))
            update(h, ahead[h])

    sweep(i, 1, own_block)

    def wide_body(jj, _):
        sweep(jj * KB_WIDE, KB_WIDE, functools.partial(past_blocks, jj * KB_WIDE, KB_WIDE))
        return 0

    lax.fori_loop(0, i // KB_WIDE, wide_body, 0)
    done = (i // KB_WIDE) * KB_WIDE
    width = KB_WIDE // 2
    while width >= 1:
        take = ((i - done) // width) * width

        @pl.when(take > 0)
        def _(done=done, width=width):
            sweep(done, width, functools.partial(past_blocks, done, width))

        done = done + take
        width //= 2

    for c in range(n_heads // 2):
        halves = []
        for h in (2 * c, 2 * c + 1):
            acc = acc_refs[h][...]
            halves.append(acc[0:HEAD_DIM] / acc[ONES_ROW:ONES_ROW + 1])
        o_ref[:, head(c)] = jnp.concatenate(halves, axis=0).T.astype(o_ref.dtype)


def _flash_attention(qt, k, vt, n_heads, batch, seq, name):
    nq = seq // TQ
    width = n_heads * LANES
    return pl.pallas_call(
        functools.partial(_flash_body, n_heads),
        grid=(batch, nq),
        in_specs=[pl.BlockSpec((None, width, TQ), lambda b, i: (b * nq + i, 0, 0)),
                  pl.BlockSpec((seq, width), lambda b, i: (b, 0)),
                  pl.BlockSpec((nq, width, TQ), lambda b, i: (b, 0, 0))],
        out_specs=pl.BlockSpec((TQ, n_heads * HEAD_DIM), lambda b, i: (b * nq + i, 0)),
        out_shape=jax.ShapeDtypeStruct((batch * seq, n_heads * HEAD_DIM), BF16),
        scratch_shapes=[pltpu.VMEM((1, TQ), F32)] * n_heads + [pltpu.VMEM((LANES, TQ), F32)] * n_heads,
        compiler_params=_cparams("parallel", "arbitrary"),
        name=name,
    )(qt, k, vt)


def _dil_body(seq, q_ref, k_ref, v_ref, o_ref, ob_ref, lse_ref, ks_ref, vt_ref):
    w = DIL_W_UNITS
    n_all = seq // w
    frow = lax.broadcasted_iota(jnp.int32, (LANES, w), 0)
    key = lax.broadcasted_iota(jnp.int32, (2 * w, w), 0)
    qry = lax.broadcasted_iota(jnp.int32, (2 * w, w), 1)
    cur_ok = (key >= w) & (key - w <= qry)
    prev_ok = (key < w) & (key >= qry)
    zeros_half = jnp.zeros((LANES - HEAD_DIM, w), F32)
    ks_ref[0:w, :] = jnp.zeros((w, LANES), BF16)
    for h in range(2):
        vt_ref[h, 0] = jnp.zeros((LANES, w), BF16)
    for bi, (_, dil) in enumerate(DIL_PAIRS):
        n_blk = seq // (dil * w)

        def rows_of(idx, dil=dil, n_blk=n_blk):
            start = idx // n_blk + (idx % n_blk) * (w * dil)
            return pl.ds(start, w, stride=dil) if dil > 1 else pl.ds(pl.multiple_of(start, w), w)

        def stage(idx, _, rows_of=rows_of):
            rows = rows_of(idx)
            ks_ref[pl.ds(pl.multiple_of((idx + 1) * w, w), w), :] = k_ref[rows, :].astype(BF16)
            v_t = v_ref[rows, :].T
            for h in range(2):
                v_h = jnp.concatenate([v_t[h * HEAD_DIM:(h + 1) * HEAD_DIM], zeros_half], axis=0)
                vt_ref[h, idx + 1] = jnp.where(frow == ONES_ROW, 1.0, v_h).astype(BF16)
            return 0

        lax.fori_loop(0, n_all, stage, 0, unroll=4)

        def one(idx, _, bi=bi, n_blk=n_blk, rows_of=rows_of):
            rows = rows_of(idx)
            q_t = q_ref[rows, :].T
            keys = ks_ref[pl.ds(pl.multiple_of(idx * w, w), 2 * w), :]
            visible = cur_ok | (prev_ok & (idx % n_blk > 0))
            outs, lses = [], []
            for h in range(2):
                q_h = jnp.where((frow // HEAD_DIM) == h, q_t, 0.0).astype(BF16)
                s = jnp.where(visible, jnp.dot(keys, q_h, preferred_element_type=F32), NEG_INF)
                m = jnp.max(s, axis=0, keepdims=True)
                p = jnp.exp2(s - m).astype(BF16)
                acc = (jnp.dot(vt_ref[h, idx], p[0:w], preferred_element_type=F32)
                       + jnp.dot(vt_ref[h, idx + 1], p[w:2 * w], preferred_element_type=F32))
                l = acc[ONES_ROW:ONES_ROW + 1]
                outs.append(acc[0:HEAD_DIM] / l)
                lses.append(jnp.broadcast_to(m + jnp.log2(l), (HEAD_DIM, w)))
            ob_ref[bi, rows, :] = jnp.concatenate(outs, axis=0).T
            lse_ref[bi, rows, :] = jnp.concatenate(lses, axis=0).T
            return 0

        lax.fori_loop(0, n_all, one, 0, unroll=8)

    def merge(c, _):
        rows = pl.ds(pl.multiple_of(c * TQ, TQ), TQ)
        ls = [lse_ref[b, rows, :] for b in range(len(DIL_PAIRS))]
        top = functools.reduce(jnp.maximum, ls)
        ws = [jnp.exp2(x - top) for x in ls]
        num = sum(wb * ob_ref[b, rows, :] for b, wb in enumerate(ws))
        o_ref[rows, :] = (num / sum(ws)).astype(o_ref.dtype)
        return 0

    lax.fori_loop(0, seq // TQ, merge, 0)


def _dil_attention(qc, kc, vc, batch, seq):
    blk = pl.BlockSpec((seq, LANES), lambda b, g: (b, g))
    nbr = len(DIL_PAIRS)
    return pl.pallas_call(
        functools.partial(_dil_body, seq),
        grid=(batch, DIL_HEADS // 2),
        in_specs=[blk, blk, blk],
        out_specs=blk,
        out_shape=jax.ShapeDtypeStruct((batch * seq, DIL_W), BF16),
        scratch_shapes=[pltpu.VMEM((nbr, seq, LANES), F32), pltpu.VMEM((nbr, seq, LANES), F32),
                        pltpu.VMEM((seq + DIL_W_UNITS, LANES), BF16),
                        pltpu.VMEM((2, seq // DIL_W_UNITS + 1, LANES, DIL_W_UNITS), BF16)],
        compiler_params=_cparams("parallel", "parallel"),
        name="dil_attn",
    )(qc, kc, vc)


def _layer_norm(y, g, b):
    mu = jnp.mean(y, axis=-1, keepdims=True)
    yc = y - mu
    var = jnp.mean(jnp.square(yc), axis=-1, keepdims=True)
    return yc * lax.rsqrt(var + LN_EPS) * g + b


def _first_argmax(rows):
    best, idx = rows[0], jnp.zeros(rows[0].shape, jnp.int32)
    for j in range(1, len(rows)):
        upd = rows[j] > best
        idx = jnp.where(upd, j, idx)
        best = jnp.where(upd, rows[j], best)
    return best, idx


def _pick(rows, idx):
    out = rows[0]
    for j in range(1, len(rows)):
        out = jnp.where(idx == j, rows[j], out)
    return out


def _top2(rows):
    v1, i1 = _first_argmax(rows)
    rest = [jnp.where(i1 == j, NEG_INF, r) for j, r in enumerate(rows)]
    v2, i2 = _first_argmax(rest)
    return v1, i1, v2, i2


def _outproj_body(oa_ref, ob_ref, oc_ref, wo_ref, x_ref, g_ref, b_ref, rwt_ref, rb_ref,
                  x1_ref, xp_ref, grp_ref, cnt_ref):
    i = pl.program_id(0)
    tm = x_ref.shape[0]

    @pl.when(i == 0)
    def _():
        cnt_ref[...] = jnp.zeros_like(cnt_ref)

    mix = (jnp.dot(oa_ref[...], wo_ref[0:MLA_W, :], preferred_element_type=F32)
           + jnp.dot(ob_ref[...], wo_ref[MLA_W:MLA_W + MOBA_W, :], preferred_element_type=F32)
           + jnp.dot(oc_ref[...], wo_ref[MLA_W + MOBA_W:, :], preferred_element_type=F32))
    x1 = _layer_norm(ALPHA * x_ref[...] + mix, g_ref[...], b_ref[...])
    x1_ref[...] = x1
    bits = lax.bitcast_convert_type(x1.astype(BF16).astype(F32), jnp.uint32)
    words = bits[:, 0:D_MODEL // 2] | (bits[:, D_MODEL // 2:] >> 16)
    for r in range(PACK_ROWS):
        xp_ref[pl.ds(r, tm, stride=TILE_ROWS), :] = words[:, r * LANES:(r + 1) * LANES]
    for r in range(PACK_ROWS + 1, TILE_ROWS):
        xp_ref[pl.ds(r, tm, stride=TILE_ROWS), :] = jnp.zeros((tm, LANES), jnp.uint32)

    logits = _dot_nt_split(_split_bf16(rwt_ref[...]), _split_bf16(x1))
    s = jax.nn.sigmoid(logits)
    sb = s + rb_ref[...]
    s_rows = [s[e:e + 1, :] for e in range(N_EXPERTS)]
    sb_rows = [sb[e:e + 1, :] for e in range(N_EXPERTS)]
    grp = lambda rows, g: rows[g * EXP_PER_GROUP:(g + 1) * EXP_PER_GROUP]
    g_scores = []
    for g in range(N_GROUPS):
        v1, _, v2, _ = _top2(grp(sb_rows, g))
        g_scores.append(v1 + v2)
    _, g_sel = _first_argmax(g_scores)
    in_b = [_pick([grp(sb_rows, g)[j] for g in range(N_GROUPS)], g_sel) for j in range(EXP_PER_GROUP)]
    in_s = [_pick([grp(s_rows, g)[j] for g in range(N_GROUPS)], g_sel) for j in range(EXP_PER_GROUP)]
    _, j1, _, j2 = _top2(in_b)
    s1, s2 = _pick(in_s, j1), _pick(in_s, j2)
    den = s1 + s2
    g1, g2 = s1 / den, s2 / den
    gate_rows = [jnp.where(j1 == j, g1, 0.0) + jnp.where(j2 == j, g2, 0.0) for j in range(EXP_PER_GROUP)]
    gate_rows.append(jnp.zeros((LANES - EXP_PER_GROUP, tm), F32))
    xp_ref[pl.ds(PACK_ROWS, tm, stride=TILE_ROWS), :] = lax.bitcast_convert_type(
        jnp.concatenate(gate_rows, axis=0).T, jnp.uint32)
    grp_ref[...] = jnp.concatenate([g_sel.astype(F32), jnp.zeros((7, tm), F32)], axis=0)
    gio = lax.broadcasted_iota(jnp.int32, (N_GROUPS, tm), 0)
    cnt_ref[...] += jnp.sum(jnp.where(gio == g_sel, 1.0, 0.0), axis=1, keepdims=True)


def _outproj_ln_router(o_a, o_b, o_c, w_out_b, x2, ln_g, ln_b, rwt, rb):
    t = x2.shape[0]
    tm = TM_PROJ
    row = lambda w: pl.BlockSpec((tm, w), lambda i: (i, 0))
    full = lambda a: pl.BlockSpec(a.shape, lambda i: (0,) * a.ndim)
    return pl.pallas_call(
        _outproj_body,
        grid=(t // tm,),
        in_specs=[row(MLA_W), row(MOBA_W), row(DIL_W), full(w_out_b), row(D_MODEL), full(ln_g), full(ln_b),
                  full(rwt), full(rb)],
        out_specs=[row(D_MODEL), pl.BlockSpec((tm * TILE_ROWS, LANES), lambda i: (i, 0)),
                   pl.BlockSpec((8, tm), lambda i: (0, i)),
                   pl.BlockSpec((N_GROUPS, LANES), lambda i: (0, 0))],
        out_shape=[jax.ShapeDtypeStruct((t, D_MODEL), F32),
                   jax.ShapeDtypeStruct((t * TILE_ROWS, LANES), jnp.uint32),
                   jax.ShapeDtypeStruct((8, t), F32),
                   jax.ShapeDtypeStruct((N_GROUPS, LANES), F32)],
        compiler_params=_cparams("arbitrary"),
        name="outproj_ln_router",
    )(o_a, o_b, o_c, w_out_b, x2, ln_g, ln_b, rwt, rb)


TN_SORT = 512
TM_ROWS = 1024
DMA_UNROLL = 8


def _dest_body(grp_ref, pstart_ref, dest_ref, carry_ref):
    n = grp_ref.shape[1]

    @pl.when(pl.program_id(0) == 0)
    def _():
        carry_ref[...] = jnp.broadcast_to(pstart_ref[...], carry_ref.shape)

    before = (lax.broadcasted_iota(jnp.int32, (n, n), 0) < lax.broadcasted_iota(jnp.int32, (n, n), 1))
    before = jnp.where(before, 1.0, 0.0).astype(BF16)
    gio = lax.broadcasted_iota(jnp.int32, (N_GROUPS, n), 0)
    hit = jnp.where(gio == grp_ref[0:1, :].astype(jnp.int32), 1.0, 0.0)
    earlier = jnp.dot(hit.astype(BF16), before, preferred_element_type=F32)
    dest = jnp.sum(hit * (earlier + carry_ref[:, 0:1]), axis=0, keepdims=True)
    carry_ref[...] += jnp.sum(hit, axis=1, keepdims=True)
    dest_ref[...] = dest.astype(jnp.int32)


def _dest_rows(grp, pstart):
    t = grp.shape[1]
    return pl.pallas_call(
        _dest_body,
        grid=(t // TN_SORT,),
        in_specs=[pl.BlockSpec((8, TN_SORT), lambda i: (0, i)),
                  pl.BlockSpec((N_GROUPS, 1), lambda i: (0, 0))],
        out_specs=pl.BlockSpec((1, TN_SORT), lambda i: (0, i)),
        out_shape=jax.ShapeDtypeStruct((1, t), jnp.int32),
        scratch_shapes=[pltpu.VMEM((N_GROUPS, LANES), F32)],
        compiler_params=_cparams("arbitrary"),
        name="moe_dest",
    )(grp, pstart)


def _tile(row):
    return pl.ds(pl.multiple_of(row * TILE_ROWS, TILE_ROWS), TILE_ROWS)


def _tile_row(r, n):
    return pl.ds(r, n, stride=TILE_ROWS)


def _scatter_body(dest_ref, x_ref, xs_in_ref, xs_ref, sem):
    del xs_in_ref
    tm = x_ref.shape[0] // TILE_ROWS

    def tile_copy(r):
        return pltpu.make_async_copy(x_ref.at[_tile(r)], xs_ref.at[_tile(dest_ref[0, r])], sem)

    def issue(r, _):
        tile_copy(r).start()
        return 0

    def drain(r, _):
        tile_copy(r).wait()
        return 0

    lax.fori_loop(0, tm, issue, 0, unroll=DMA_UNROLL)
    lax.fori_loop(0, tm, drain, 0, unroll=DMA_UNROLL)


def _scatter_rows(dest, xp, n_rows):
    t = xp.shape[0] // TILE_ROWS
    xs_init = jnp.zeros((n_rows * TILE_ROWS, LANES), xp.dtype)
    return pl.pallas_call(
        _scatter_body,
        grid=(t // TM_ROWS,),
        in_specs=[pl.BlockSpec((1, TM_ROWS), lambda i: (0, i), memory_space=pltpu.SMEM),
                  pl.BlockSpec((TM_ROWS * TILE_ROWS, LANES), lambda i: (i, 0)),
                  pl.BlockSpec(memory_space=pl.ANY)],
        out_specs=pl.BlockSpec(memory_space=pl.ANY),
        out_shape=jax.ShapeDtypeStruct(xs_init.shape, xp.dtype),
        scratch_shapes=[pltpu.SemaphoreType.DMA(())],
        input_output_aliases={2: 0},
        compiler_params=_cparams("arbitrary"),
        name="moe_scatter",
    )(dest, xp, xs_init)


def _expert_body(blk_grp_ref, nused_ref, xs_ref, w1_ref, w3_ref, w2_ref, y_ref):
    del blk_grp_ref
    i = pl.program_id(0)
    rows = xs_ref.shape[0] // TILE_ROWS

    @pl.when(i < nused_ref[0])
    def _():
        words = jnp.concatenate([xs_ref[_tile_row(r, rows), :] for r in range(PACK_ROWS)], axis=1)
        unpack = lambda bits: lax.bitcast_convert_type(bits, F32).astype(BF16)
        xb = jnp.concatenate([unpack(words & jnp.uint32(0xFFFF0000)), unpack(words << 16)], axis=1)
        gates = lax.bitcast_convert_type(xs_ref[_tile_row(PACK_ROWS, rows), :], F32)
        y = jnp.zeros((rows, D_MODEL), F32)
        for j in range(EXP_PER_GROUP):
            h1 = jnp.dot(xb, w1_ref[j], preferred_element_type=F32)
            h3 = jnp.dot(xb, w3_ref[j], preferred_element_type=F32)
            hb = (jax.nn.silu(h1) * h3).astype(BF16)
            y = y + jnp.dot(hb, w2_ref[j], preferred_element_type=F32) * gates[:, j:j + 1]
        for r in range(TILE_ROWS):
            y_ref[_tile_row(r, rows), :] = y[:, r * LANES:(r + 1) * LANES]

    @pl.when(i >= nused_ref[0])
    def _():
        y_ref[...] = jnp.zeros_like(y_ref)


def _expert_ffn(blk_grp, nused, xs, w1, w3, w2, layer):
    n_blk = xs.shape[0] // (MOE_BLOCK * TILE_ROWS)
    d = D_MODEL
    grp_w = lambda r, c: pl.BlockSpec((None, None, EXP_PER_GROUP, r, c),
                                      lambda i, bg, nu: (layer, bg[i], 0, 0, 0))
    tiles = pl.BlockSpec((MOE_BLOCK * TILE_ROWS, LANES), lambda i, bg, nu: (i, 0))
    grid_spec = pltpu.PrefetchScalarGridSpec(
        num_scalar_prefetch=2,
        grid=(n_blk,),
        in_specs=[tiles, grp_w(d, D_EXPERT), grp_w(d, D_EXPERT), grp_w(D_EXPERT, d)],
        out_specs=tiles,
    )
    return pl.pallas_call(
        _expert_body,
        grid_spec=grid_spec,
        out_shape=jax.ShapeDtypeStruct(xs.shape, F32),
        compiler_params=_cparams("arbitrary"),
        name="moe_experts",
    )(blk_grp, nused, xs, w1, w3, w2)


def _combine_body(dest_ref, dest_next_ref, x1_ref, g_ref, b_ref, ys_ref, o_ref, buf_ref, sems):
    i = pl.program_id(0)
    n_steps = pl.num_programs(0)
    tm = x1_ref.shape[0]

    def tile_copy(idx_ref, slot, r):
        return pltpu.make_async_copy(ys_ref.at[_tile(idx_ref[0, r])], buf_ref.at[slot, _tile(r)], sems.at[slot])

    def start_tiles(idx_ref, slot):
        def issue(r, _):
            tile_copy(idx_ref, slot, r).start()
            return 0
        lax.fori_loop(0, tm, issue, 0, unroll=DMA_UNROLL)

    @pl.when(i == 0)
    def _():
        start_tiles(dest_ref, 0)

    @pl.when(i + 1 < n_steps)
    def _():
        start_tiles(dest_next_ref, (i + 1) % 2)

    slot = i % 2

    def drain(r, _):
        tile_copy(dest_ref, slot, r).wait()
        return 0

    lax.fori_loop(0, tm, drain, 0, unroll=DMA_UNROLL)
    ffn = jnp.concatenate([buf_ref[slot, _tile_row(r, tm), :] for r in range(TILE_ROWS)], axis=1)
    o_ref[...] = _layer_norm(ALPHA * x1_ref[...] + ffn, g_ref[...], b_ref[...])


def _combine_ln(dest, x1, ln_g, ln_b, ys):
    t, d = x1.shape
    full = lambda a: pl.BlockSpec(a.shape, lambda i: (0,) * a.ndim)
    n_steps = t // TM_ROWS
    return pl.pallas_call(
        _combine_body,
        grid=(n_steps,),
        in_specs=[pl.BlockSpec((1, TM_ROWS), lambda i: (0, i), memory_space=pltpu.SMEM),
                  pl.BlockSpec((1, TM_ROWS), lambda i: (0, jnp.minimum(i + 1, n_steps - 1)),
                               memory_space=pltpu.SMEM),
                  pl.BlockSpec((TM_ROWS, d), lambda i: (i, 0)), full(ln_g), full(ln_b),
                  pl.BlockSpec(memory_space=pl.ANY)],
        out_specs=pl.BlockSpec((TM_ROWS, d), lambda i: (i, 0)),
        out_shape=jax.ShapeDtypeStruct((t, d), F32),
        scratch_shapes=[pltpu.VMEM((2, TM_ROWS * TILE_ROWS, LANES), F32), pltpu.SemaphoreType.DMA((2,))],
        compiler_params=_cparams("arbitrary"),
        name="moe_combine_ln",
    )(dest, dest, x1, ln_g, ln_b, ys)


def _moe_layer(x1, xp, grp, counts, w1, w3, w2, layer, ln_g, ln_b):
    t = x1.shape[0]
    n_rows = t + N_GROUPS * MOE_BLOCK
    n_blk = n_rows // MOE_BLOCK
    cnt = counts[:, 0].astype(jnp.int32)
    padded = (cnt + MOE_BLOCK - 1) // MOE_BLOCK * MOE_BLOCK
    pend = jnp.cumsum(padded)
    pstart = (pend - padded).astype(F32).reshape(N_GROUPS, 1)
    blk_row0 = jnp.arange(n_blk, dtype=jnp.int32) * MOE_BLOCK
    blk_grp = jnp.minimum(jnp.sum((pend[None, :] <= blk_row0[:, None]).astype(jnp.int32), axis=1), N_GROUPS - 1)
    nused = (pend[-1:] // MOE_BLOCK).astype(jnp.int32)
    dest = _dest_rows(grp, pstart)
    xs = _scatter_rows(dest, xp, n_rows)
    ys = _expert_ffn(blk_grp, nused, xs, w1, w3, w2, layer)
    return _combine_ln(dest, x1, ln_g, ln_b, ys)


def _rope_tables(seq):
    def tab(dim):
        inv = ROPE_THETA ** (-jnp.arange(0, dim, 2, dtype=F32) / dim)
        ang = jnp.arange(seq, dtype=F32)[:, None] * inv[None, :]
        return jnp.cos(ang), jnp.sin(ang)

    c, s = tab(HEAD_DIM)
    cos64 = jnp.tile(jnp.concatenate([c, c], axis=1), (1, LANES // HEAD_DIM))
    sin64 = jnp.tile(jnp.concatenate([-s, s], axis=1), (1, LANES // HEAD_DIM))
    c, s = tab(MLA_ROPE)
    pad_l, pad_r = MLA_ROPE_LANE, LANES - MLA_ROPE_LANE - MLA_ROPE
    cosm = jnp.concatenate([jnp.ones((seq, pad_l), F32), c, c, jnp.ones((seq, pad_r), F32)], axis=1)
    sinm = jnp.concatenate([jnp.zeros((seq, pad_l), F32), -s, s, jnp.zeros((seq, pad_r), F32)], axis=1)
    return cos64, sin64, cosm, sinm


def _prep_mixer_weights(w_in, q_norm, w_uq, kv_norm, w_ukv):
    d = w_in.shape[0]
    lat = MLA_Q_LORA + MLA_KV_LORA
    w_in_p = jnp.concatenate([
        w_in[:, :lat], jnp.zeros((d, MLA_ROPE_LANE), F32), w_in[:, lat:lat + MLA_ROPE],
        jnp.zeros((d, LANES - MLA_ROPE_LANE - MLA_ROPE), F32), w_in[:, lat + MLA_ROPE:]], axis=1).astype(BF16)
    wq = w_uq.reshape(MLA_Q_LORA, MLA_HEADS, MLA_NOPE + MLA_ROPE)
    wuq_p = jnp.pad(wq, ((0, 0), (0, 0), (0, MLA_HEAD_PAD - MLA_NOPE - MLA_ROPE)))
    wuq_p = wuq_p.reshape(MLA_Q_LORA, MLA_HEADS * MLA_HEAD_PAD).astype(BF16)
    wkv = w_ukv.reshape(MLA_KV_LORA, MLA_HEADS, MLA_NOPE + MLA_V)
    wk = jnp.pad(wkv[:, :, :MLA_NOPE], ((0, 0), (0, 0), (0, MLA_HEAD_PAD - MLA_NOPE)))
    wukv_p = jnp.concatenate([wk.reshape(MLA_KV_LORA, MLA_HEADS * MLA_HEAD_PAD),
                              wkv[:, :, MLA_NOPE:].reshape(MLA_KV_LORA, MLA_W)], axis=1).astype(BF16)
    return w_in_p, q_norm.reshape(1, -1), wuq_p, kv_norm.reshape(1, -1), wukv_p


def _mixer_heads(x2, w_in, q_norm, w_uq, kv_norm, w_ukv, tabs, batch, seq):
    w_in_p, qn, wuq_p, kvn, wukv_p = _prep_mixer_weights(w_in, q_norm, w_uq, kv_norm, w_ukv)
    qm, km, vm, qb, kb, vb, qc, kc, vc = _inproj(x2, w_in_p, qn, kvn, wuq_p, wukv_p, tabs, seq)
    o_a = _flash_attention(qm, km, vm, MLA_HEADS, batch, seq, "mla_attn")
    o_b = _flash_attention(qb, kb, vb, MOBA_HEADS, batch, seq, "moba_attn")
    o_c = _dil_attention(qc, kc, vc, batch, seq)
    return o_a, o_b, o_c


def kernel(x, w_in, mla_q_norm, mla_w_uq, mla_kv_norm, mla_w_ukv, w_out, ln1_g, ln1_b, router_w, router_b, moe_w1, moe_w3, moe_w2, ln2_g, ln2_b):
    batch, seq, d = x.shape
    tabs = _rope_tables(seq)
    x2 = x.reshape(batch * seq, d)
    rwt = router_w.T
    rb = router_b.reshape(N_EXPERTS, 1)
    by_group = lambda w: w.astype(BF16).reshape(DEPTH, N_GROUPS, EXP_PER_GROUP, *w.shape[2:])
    w1g, w3g, w2g = by_group(moe_w1), by_group(moe_w3), by_group(moe_w2)
    for l in range(DEPTH):
        o_a, o_b, o_c = _mixer_heads(x2, w_in[l], mla_q_norm[l], mla_w_uq[l], mla_kv_norm[l], mla_w_ukv[l],
                                     tabs, batch, seq)
        x1, xp, grp, counts = _outproj_ln_router(o_a, o_b, o_c, w_out[l].astype(BF16), x2,
                                                 ln1_g[l].reshape(1, d), ln1_b[l].reshape(1, d), rwt, rb)
        x2 = _moe_layer(x1, xp, grp, counts, w1g, w3g, w2g, l,
                        ln2_g[l].reshape(1, d), ln2_b[l].reshape(1, d))
    return x2.reshape(batch, seq, d)
```

```python
import functools

import jax
import jax.numpy as jnp
import numpy as np
from jax import lax
from jax.experimental import pallas as pl
from jax.experimental.pallas import tpu as pltpu

F32 = jnp.float32
BF16 = jnp.bfloat16
LANES = 128
NEG_INF = float("-inf")
VMEM_LIMIT_BYTES = 56 * 1024 * 1024

D_MODEL = 1024
DEPTH = 2
HEAD_DIM = 64
ROPE_THETA = 10000.0
MLA_HEADS = 4
MLA_NOPE = 64
MLA_ROPE = 32
MLA_V = 64
MLA_Q_LORA = 256
MLA_KV_LORA = 128
MOBA_HEADS = 6
MOBA_BLOCK = 256
MOBA_TOPK = 3
DIL_HEADS = 6
DIL_PAIRS = ((128, 1), (512, 4), (2048, 16))
MLA_W = MLA_HEADS * MLA_V
MOBA_W = MOBA_HEADS * HEAD_DIM
DIL_W = DIL_HEADS * HEAD_DIM
N_EXPERTS = 64
N_GROUPS = 8
EXP_PER_GROUP = N_EXPERTS // N_GROUPS
TOP_K = 2
D_EXPERT = 256
MOE_BLOCK = 512
ALPHA = (2 * DEPTH) ** 0.25
LN_EPS = 1e-5
RMS_EPS = 1e-6

SEG_A = 512
SEG_B = 3 * MOBA_W
SEG_C = 3 * DIL_W
N_IN_PAD = SEG_A + SEG_B + SEG_C
MLA_HEAD_PAD = 128
MLA_ROPE_LANE = 64

TILE_ROWS = 8
PACK_ROWS = D_MODEL // 2 // 128
TM_PROJ = 512
TQ = 256
SCORES_AHEAD = 6
KB_WIDE = 4
DIL_W_UNITS = 128
LOG2_E = 1.4426950408889634
ONES_ROW = HEAD_DIM
SEL_LANE = HEAD_DIM
MASK_BIG = 2.0 ** 100


def _cparams(*sem):
    return pltpu.CompilerParams(dimension_semantics=sem, vmem_limit_bytes=VMEM_LIMIT_BYTES)


def _rope(x, cos, sin_signed, half):
    lane = lax.broadcasted_iota(jnp.int32, x.shape, 1)
    first = ((lane // half) % 2) == 0
    rot = jnp.where(first, pltpu.roll(x, LANES - half, 1), pltpu.roll(x, half, 1))
    return x * cos + rot * sin_signed


def _split_bf16(x):
    hi = x.astype(BF16)
    return hi, (x - hi.astype(F32)).astype(BF16)


def _dot_nt_split(a_parts, b_parts):
    dn = (((1,), (1,)), ((), ()))
    dot = lambda x, y: lax.dot_general(x, y, dn, preferred_element_type=F32)
    (a_hi, a_lo), (b_hi, b_lo) = a_parts, b_parts
    return dot(a_hi, b_hi) + (dot(a_hi, b_lo) + dot(a_lo, b_hi))


def _rms(x, g):
    return x * lax.rsqrt(jnp.mean(jnp.square(x), axis=-1, keepdims=True) + RMS_EPS) * g


def _inproj_body(seq_blocks, x_ref, w_ref, qn_ref, kvn_ref, wuq_ref, wukv_ref,
                 cos64_ref, sin64_ref, cosm_ref, sinm_ref,
                 qm_ref, km_ref, vm_ref, qb_ref, kb_ref, vb_ref, qc_ref, kc_ref, vc_ref,
                 kmean_ref):
    i = pl.program_id(0)
    tm = x_ref.shape[0]
    n_kb = kmean_ref.shape[0]
    sblk = i % seq_blocks

    @pl.when(sblk == 0)
    def _():
        kmean_ref[...] = jnp.zeros_like(kmean_ref)

    xb = x_ref[...].astype(BF16)
    cos64, sin64 = cos64_ref[...], sin64_ref[...]
    cosm, sinm = cosm_ref[...], sinm_ref[...]

    h_a = jnp.dot(xb, w_ref[:, 0:SEG_A], preferred_element_type=F32)
    cq = _rms(h_a[:, 0:MLA_Q_LORA], qn_ref[...]).astype(BF16)
    ckv = _rms(h_a[:, MLA_Q_LORA:MLA_Q_LORA + MLA_KV_LORA], kvn_ref[...]).astype(BF16)
    kr = _rope(h_a[:, MLA_Q_LORA + MLA_KV_LORA:SEG_A], cosm, sinm, MLA_ROPE // 2)
    q_m = jnp.dot(cq, wuq_ref[...], preferred_element_type=F32)
    kv_m = jnp.dot(ckv, wukv_ref[...], preferred_element_type=F32)
    mla_scale = (MLA_NOPE + MLA_ROPE) ** -0.5 * LOG2_E
    lane_t = lax.broadcasted_iota(jnp.int32, (tm, LANES), 1)
    row_t = lax.broadcasted_iota(jnp.int32, (LANES, tm), 0)
    zeros_half = jnp.zeros((LANES - HEAD_DIM, tm), F32)

    def put_t(ref, h, val_t):
        for b in range(tm // TQ):
            ref[b, h * LANES:(h + 1) * LANES, :] = val_t[:, b * TQ:(b + 1) * TQ].astype(ref.dtype)

    def v_heads_t(pair):
        pt = pair.T
        return [jnp.where(row_t == ONES_ROW, 1.0, jnp.concatenate([pt[s * HEAD_DIM:(s + 1) * HEAD_DIM], zeros_half], 0))
                for s in range(2)]

    for h in range(MLA_HEADS):
        sl = slice(h * MLA_HEAD_PAD, (h + 1) * MLA_HEAD_PAD)
        put_t(qm_ref, h, (_rope(q_m[:, sl], cosm, sinm, MLA_ROPE // 2) * mla_scale).T)
        km_ref[:, sl] = (kv_m[:, sl] + kr).astype(BF16)
    for c in range(MLA_HEADS // 2):
        v_pair = kv_m[:, (MLA_HEADS + c) * LANES:(MLA_HEADS + c + 1) * LANES]
        for s, v_t in enumerate(v_heads_t(v_pair)):
            put_t(vm_ref, 2 * c + s, v_t)

    h_b = jnp.dot(xb, w_ref[:, SEG_A:SEG_A + SEG_B], preferred_element_type=F32)
    scale = HEAD_DIM ** -0.5
    lane = lax.broadcasted_iota(jnp.int32, (n_kb, LANES), 1)
    pos = sblk * tm + lax.broadcasted_iota(jnp.int32, (1, tm), 1)
    q_blk = pos // MOBA_BLOCK
    jio = lax.broadcasted_iota(jnp.int32, (n_kb, tm), 0)
    elig = jio < q_blk
    row_blk = (sblk * tm + lax.broadcasted_iota(jnp.int32, (tm, LANES), 0)) // MOBA_BLOCK
    blk_lane = lane_t == SEL_LANE + row_blk
    first_head = lane_t < HEAD_DIM

    def split(pair):
        return (jnp.where(first_head, pair, 0.0), jnp.where(first_head, pltpu.roll(pair, HEAD_DIM, 1), 0.0))

    for c in range(MOBA_W // LANES):
        sl = slice(c * LANES, (c + 1) * LANES)
        q = _rope(h_b[:, sl], cos64, sin64, HEAD_DIM // 2)
        k = _rope(h_b[:, MOBA_W + c * LANES:MOBA_W + (c + 1) * LANES], cos64, sin64, HEAD_DIM // 2)
        km = kmean_ref[:, sl]
        kb_row = lax.broadcasted_iota(jnp.int32, (n_kb, LANES), 0)
        for b in range(tm // MOBA_BLOCK):
            mean_b = jnp.mean(k[b * MOBA_BLOCK:(b + 1) * MOBA_BLOCK], axis=0, keepdims=True)
            km = jnp.where(kb_row == sblk * (tm // MOBA_BLOCK) + b, mean_b, km)
        kmean_ref[:, sl] = km
        q_t = (q * (scale * LOG2_E)).T
        k_heads = split(k)
        q_parts = _split_bf16(q)
        v_t = v_heads_t(h_b[:, 2 * MOBA_W + c * LANES:2 * MOBA_W + (c + 1) * LANES])
        for half in range(2):
            hs = slice((2 * c + half) * LANES, (2 * c + half + 1) * LANES)
            in_head = (lane // HEAD_DIM) == half
            gate = _dot_nt_split(_split_bf16(jnp.where(in_head, km, 0.0)), q_parts)
            gate = jnp.where(elig, gate, NEG_INF)
            q_rows = [q_t[half * HEAD_DIM:(half + 1) * HEAD_DIM]]
            for j in range(n_kb):
                gj = gate[j:j + 1, :]
                beats = (gate > gj) | ((gate == gj) & (jio < j))
                cnt = jnp.sum(beats.astype(F32), axis=0, keepdims=True)
                keep = ((cnt < MOBA_TOPK) & elig[j:j + 1, :]) | (q_blk == j)
                q_rows.append(jnp.where(keep, 0.0, -MASK_BIG))
            q_rows.append(jnp.zeros((LANES - SEL_LANE - n_kb, tm), F32))
            put_t(qb_ref, 2 * c + half, jnp.concatenate(q_rows, axis=0))
            kb_ref[:, hs] = jnp.where(blk_lane, 1.0, k_heads[half]).astype(BF16)
            put_t(vb_ref, 2 * c + half, v_t[half])

    h_c = jnp.dot(xb, w_ref[:, SEG_A + SEG_B:], preferred_element_type=F32)
    for c in range(DIL_W // LANES):
        sl = slice(c * LANES, (c + 1) * LANES)
        qc_ref[:, sl] = _rope(h_c[:, sl], cos64, sin64, HEAD_DIM // 2) * (scale * LOG2_E)
        kc_ref[:, sl] = _rope(h_c[:, DIL_W + c * LANES:DIL_W + (c + 1) * LANES], cos64, sin64, HEAD_DIM // 2)
    vc_ref[...] = h_c[:, 2 * DIL_W:]


def _inproj(x2, w_in_p, qn, kvn, wuq_p, wukv_p, tabs, seq):
    t = x2.shape[0]
    tm = TM_PROJ
    seq_blocks = seq // tm
    n_kb = seq // MOBA_BLOCK
    row = lambda w: pl.BlockSpec((tm, w), lambda i: (i, 0))
    full = lambda a: pl.BlockSpec(a.shape, lambda i: (0,) * a.ndim)
    tab = pl.BlockSpec((tm, LANES), lambda i: (i % seq_blocks, 0))
    out_shapes = [
        jax.ShapeDtypeStruct((t // TQ, MLA_HEADS * LANES, TQ), BF16),
        jax.ShapeDtypeStruct((t, MLA_HEADS * MLA_HEAD_PAD), BF16),
        jax.ShapeDtypeStruct((t // TQ, MLA_HEADS * LANES, TQ), BF16),
        jax.ShapeDtypeStruct((t // TQ, MOBA_HEADS * LANES, TQ), BF16),
        jax.ShapeDtypeStruct((t, MOBA_HEADS * LANES), BF16),
        jax.ShapeDtypeStruct((t // TQ, MOBA_HEADS * LANES, TQ), BF16),
        jax.ShapeDtypeStruct((t, DIL_W), F32),
        jax.ShapeDtypeStruct((t, DIL_W), F32),
        jax.ShapeDtypeStruct((t, DIL_W), F32),
    ]
    return pl.pallas_call(
        functools.partial(_inproj_body, seq_blocks),
        grid=(t // tm,),
        in_specs=[row(D_MODEL), full(w_in_p), full(qn), full(kvn), full(wuq_p), full(wukv_p),
                  tab, tab, tab, tab],
        out_specs=[row(s.shape[1]) if len(s.shape) == 2 else
                   pl.BlockSpec((tm // TQ,) + s.shape[1:], lambda i: (i, 0, 0)) for s in out_shapes],
        out_shape=out_shapes,
        scratch_shapes=[pltpu.VMEM((n_kb, MOBA_W), F32)],
        compiler_params=_cparams("arbitrary"),
        name="inproj",
    )(x2, w_in_p, qn, kvn, wuq_p, wukv_p, *tabs)


def _flash_body(n_heads, qt_ref, k_ref, vt_ref, o_ref, *state):
    i = pl.program_id(1)
    tq = qt_ref.shape[1]

    def head(h):
        return slice(h * LANES, (h + 1) * LANES)

    m_refs, acc_refs = state[:n_heads], state[n_heads:]
    key_le_query = (lax.broadcasted_iota(jnp.int32, (tq, tq), 0) <= lax.broadcasted_iota(jnp.int32, (tq, tq), 1))

    def scores(j, n, h):
        rows = pl.ds(pl.multiple_of(j * tq, tq), n * tq)
        return jnp.dot(k_ref[rows, head(h)], qt_ref[head(h), :], preferred_element_type=F32)

    def pv(j, n, h, p):
        return sum(jnp.dot(vt_ref[j + b, head(h), :], p[b * tq:(b + 1) * tq], preferred_element_type=F32)
                   for b in range(n))

    def own_block(h, s):
        s = jnp.where(key_le_query, s, NEG_INF)
        m = jnp.max(s, axis=0, keepdims=True)
        acc_refs[h][...] = pv(i, 1, h, jnp.exp2(s - m).astype(BF16))
        m_refs[h][...] = m

    def past_blocks(j, n, h, s):
        m_old = m_refs[h][...]
        m_new = jnp.maximum(m_old, jnp.max(s, axis=0, keepdims=True))
        p = jnp.exp2(s - m_new).astype(BF16)
        acc_refs[h][...] = jnp.exp2(m_old - m_new) * acc_refs[h][...] + pv(j, n, h, p)
        m_refs[h][...] = m_new

    def sweep(j, n, update):
        ahead = [scores(j, n, h) for h in range(min(SCORES_AHEAD, n_heads))]
        for h in range(n_heads):
            if h + SCORES_AHEAD < n_heads:
                ahead.append(scores(j, n, h + SCORES_AHEAD))
            update(h, ahead[h])

    sweep(i, 1, own_block)

    def wide_body(jj, _):
        sweep(jj * KB_WIDE, KB_WIDE, functools.partial(past_blocks, jj * KB_WIDE, KB_WIDE))
        return 0

    lax.fori_loop(0, i // KB_WIDE, wide_body, 0)
    done = (i // KB_WIDE) * KB_WIDE
    width = KB_WIDE // 2
    while width >= 1:
        take = ((i - done) // width) * width

        @pl.when(take > 0)
        def _(done=done, width=width):
            sweep(done, width, functools.partial(past_blocks, done, width))

        done = done + take
        width //= 2

    for c in range(n_heads // 2):
        halves = []
        for h in (2 * c, 2 * c + 1):
            acc = acc_refs[h][...]
            halves.append(acc[0:HEAD_DIM] / acc[ONES_ROW:ONES_ROW + 1])
        o_ref[:, head(c)] = jnp.concatenate(halves, axis=0).T.astype(o_ref.dtype)


def _flash_attention(qt, k, vt, n_heads, batch, seq, name):
    nq = seq // TQ
    width = n_heads * LANES
    return pl.pallas_call(
        functools.partial(_flash_body, n_heads),
        grid=(batch, nq),
        in_specs=[pl.BlockSpec((None, width, TQ), lambda b, i: (b * nq + i, 0, 0)),
                  pl.BlockSpec((seq, width), lambda b, i: (b, 0)),
                  pl.BlockSpec((nq, width, TQ), lambda b, i: (b, 0, 0))],
        out_specs=pl.BlockSpec((TQ, n_heads * HEAD_DIM), lambda b, i: (b * nq + i, 0)),
        out_shape=jax.ShapeDtypeStruct((batch * seq, n_heads * HEAD_DIM), BF16),
        scratch_shapes=[pltpu.VMEM((1, TQ), F32)] * n_heads + [pltpu.VMEM((LANES, TQ), F32)] * n_heads,
        compiler_params=_cparams("parallel", "arbitrary"),
        name=name,
    )(qt, k, vt)


def _dil_body(seq, q_ref, k_ref, v_ref, o_ref, ob_ref, lse_ref, ks_ref, vt_ref):
    w = DIL_W_UNITS
    n_all = seq // w
    frow = lax.broadcasted_iota(jnp.int32, (LANES, w), 0)
    key = lax.broadcasted_iota(jnp.int32, (2 * w, w), 0)
    qry = lax.broadcasted_iota(jnp.int32, (2 * w, w), 1)
    cur_ok = (key >= w) & (key - w <= qry)
    prev_ok = (key < w) & (key >= qry)
    zeros_half = jnp.zeros((LANES - HEAD_DIM, w), F32)
    ks_ref[0:w, :] = jnp.zeros((w, LANES), BF16)
    for h in range(2):
        vt_ref[h, 0] = jnp.zeros((LANES, w), BF16)
    for bi, (_, dil) in enumerate(DIL_PAIRS):
        n_blk = seq // (dil * w)

        def rows_of(idx, dil=dil, n_blk=n_blk):
            start = idx // n_blk + (idx % n_blk) * (w * dil)
            return pl.ds(start, w, stride=dil) if dil > 1 else pl.ds(pl.multiple_of(start, w), w)

        def stage(idx, _, rows_of=rows_of):
            rows = rows_of(idx)
            ks_ref[pl.ds(pl.multiple_of((idx + 1) * w, w), w), :] = k_ref[rows, :].astype(BF16)
            v_t = v_ref[rows, :].T
            for h in range(2):
                v_h = jnp.concatenate([v_t[h * HEAD_DIM:(h + 1) * HEAD_DIM], zeros_half], axis=0)
                vt_ref[h, idx + 1] = jnp.where(frow == ONES_ROW, 1.0, v_h).astype(BF16)
            return 0

        lax.fori_loop(0, n_all, stage, 0, unroll=4)

        def one(idx, _, bi=bi, n_blk=n_blk, rows_of=rows_of):
            rows = rows_of(idx)
            q_t = q_ref[rows, :].T
            keys = ks_ref[pl.ds(pl.multiple_of(idx * w, w), 2 * w), :]
            visible = cur_ok | (prev_ok & (idx % n_blk > 0))
            outs, lses = [], []
            for h in range(2):
                q_h = jnp.where((frow // HEAD_DIM) == h, q_t, 0.0).astype(BF16)
                s = jnp.where(visible, jnp.dot(keys, q_h, preferred_element_type=F32), NEG_INF)
                m = jnp.max(s, axis=0, keepdims=True)
                p = jnp.exp2(s - m).astype(BF16)
                acc = (jnp.dot(vt_ref[h, idx], p[0:w], preferred_element_type=F32)
                       + jnp.dot(vt_ref[h, idx + 1], p[w:2 * w], preferred_element_type=F32))
                l = acc[ONES_ROW:ONES_ROW + 1]
                outs.append(acc[0:HEAD_DIM] / l)
                lses.append(jnp.broadcast_to(m + jnp.log2(l), (HEAD_DIM, w)))
            ob_ref[bi, rows, :] = jnp.concatenate(outs, axis=0).T
            lse_ref[bi, rows, :] = jnp.concatenate(lses, axis=0).T
            return 0

        lax.fori_loop(0, n_all, one, 0, unroll=8)

    def merge(c, _):
        rows = pl.ds(pl.multiple_of(c * TQ, TQ), TQ)
        ls = [lse_ref[b, rows, :] for b in range(len(DIL_PAIRS))]
        top = functools.reduce(jnp.maximum, ls)
        ws = [jnp.exp2(x - top) for x in ls]
        num = sum(wb * ob_ref[b, rows, :] for b, wb in enumerate(ws))
        o_ref[rows, :] = (num / sum(ws)).astype(o_ref.dtype)
        return 0

    lax.fori_loop(0, seq // TQ, merge, 0)


def _dil_attention(qc, kc, vc, batch, seq):
    blk = pl.BlockSpec((seq, LANES), lambda b, g: (b, g))
    nbr = len(DIL_PAIRS)
    return pl.pallas_call(
        functools.partial(_dil_body, seq),
        grid=(batch, DIL_HEADS // 2),
        in_specs=[blk, blk, blk],
        out_specs=blk,
        out_shape=jax.ShapeDtypeStruct((batch * seq, DIL_W), BF16),
        scratch_shapes=[pltpu.VMEM((nbr, seq, LANES), F32), pltpu.VMEM((nbr, seq, LANES), F32),
                        pltpu.VMEM((seq + DIL_W_UNITS, LANES), BF16),
                        pltpu.VMEM((2, seq // DIL_W_UNITS + 1, LANES, DIL_W_UNITS), BF16)],
        compiler_params=_cparams("parallel", "parallel"),
        name="dil_attn",
    )(qc, kc, vc)


def _layer_norm(y, g, b):
    mu = jnp.mean(y, axis=-1, keepdims=True)
    yc = y - mu
    var = jnp.mean(jnp.square(yc), axis=-1, keepdims=True)
    return yc * lax.rsqrt(var + LN_EPS) * g + b


def _first_argmax(rows):
    best, idx = rows[0], jnp.zeros(rows[0].shape, jnp.int32)
    for j in range(1, len(rows)):
        upd = rows[j] > best
        idx = jnp.where(upd, j, idx)
        best = jnp.where(upd, rows[j], best)
    return best, idx


def _pick(rows, idx):
    out = rows[0]
    for j in range(1, len(rows)):
        out = jnp.where(idx == j, rows[j], out)
    return out


def _top2(rows):
    v1, i1 = _first_argmax(rows)
    rest = [jnp.where(i1 == j, NEG_INF, r) for j, r in enumerate(rows)]
    v2, i2 = _first_argmax(rest)
    return v1, i1, v2, i2


def _outproj_body(oa_ref, ob_ref, oc_ref, wo_ref, x_ref, g_ref, b_ref, rwt_ref, rb_ref,
                  x1_ref, xp_ref, grp_ref, cnt_ref):
    i = pl.program_id(0)
    tm = x_ref.shape[0]

    @pl.when(i == 0)
    def _():
        cnt_ref[...] = jnp.zeros_like(cnt_ref)

    mix = (jnp.dot(oa_ref[...], wo_ref[0:MLA_W, :], preferred_element_type=F32)
           + jnp.dot(ob_ref[...], wo_ref[MLA_W:MLA_W + MOBA_W, :], preferred_element_type=F32)
           + jnp.dot(oc_ref[...], wo_ref[MLA_W + MOBA_W:, :], preferred_element_type=F32))
    x1 = _layer_norm(ALPHA * x_ref[...] + mix, g_ref[...], b_ref[...])
    x1_ref[...] = x1
    bits = lax.bitcast_convert_type(x1.astype(BF16).astype(F32), jnp.uint32)
    words = bits[:, 0:D_MODEL // 2] | (bits[:, D_MODEL // 2:] >> 16)
    for r in range(PACK_ROWS):
        xp_ref[pl.ds(r, tm, stride=TILE_ROWS), :] = words[:, r * LANES:(r + 1) * LANES]
    for r in range(PACK_ROWS + 1, TILE_ROWS):
        xp_ref[pl.ds(r, tm, stride=TILE_ROWS), :] = jnp.zeros((tm, LANES), jnp.uint32)

    logits = _dot_nt_split(_split_bf16(rwt_ref[...]), _split_bf16(x1))
    s = jax.nn.sigmoid(logits)
    sb = s + rb_ref[...]
    s_rows = [s[e:e + 1, :] for e in range(N_EXPERTS)]
    sb_rows = [sb[e:e + 1, :] for e in range(N_EXPERTS)]
    grp = lambda rows, g: rows[g * EXP_PER_GROUP:(g + 1) * EXP_PER_GROUP]
    g_scores = []
    for g in range(N_GROUPS):
        v1, _, v2, _ = _top2(grp(sb_rows, g))
        g_scores.append(v1 + v2)
    _, g_sel = _first_argmax(g_scores)
    in_b = [_pick([grp(sb_rows, g)[j] for g in range(N_GROUPS)], g_sel) for j in range(EXP_PER_GROUP)]
    in_s = [_pick([grp(s_rows, g)[j] for g in range(N_GROUPS)], g_sel) for j in range(EXP_PER_GROUP)]
    _, j1, _, j2 = _top2(in_b)
    s1, s2 = _pick(in_s, j1), _pick(in_s, j2)
    den = s1 + s2
    g1, g2 = s1 / den, s2 / den
    gate_rows = [jnp.where(j1 == j, g1, 0.0) + jnp.where(j2 == j, g2, 0.0) for j in range(EXP_PER_GROUP)]
    gate_rows.append(jnp.zeros((LANES - EXP_PER_GROUP, tm), F32))
    xp_ref[pl.ds(PACK_ROWS, tm, stride=TILE_ROWS), :] = lax.bitcast_convert_type(
        jnp.concatenate(gate_rows, axis=0).T, jnp.uint32)
    grp_ref[...] = jnp.concatenate([g_sel.astype(F32), jnp.zeros((7, tm), F32)], axis=0)
    gio = lax.broadcasted_iota(jnp.int32, (N_GROUPS, tm), 0)
    cnt_ref[...] += jnp.sum(jnp.where(gio == g_sel, 1.0, 0.0), axis=1, keepdims=True)


def _outproj_ln_router(o_a, o_b, o_c, w_out_b, x2, ln_g, ln_b, rwt, rb):
    t = x2.shape[0]
    tm = TM_PROJ
    row = lambda w: pl.BlockSpec((tm, w), lambda i: (i, 0))
    full = lambda a: pl.BlockSpec(a.shape, lambda i: (0,) * a.ndim)
    return pl.pallas_call(
        _outproj_body,
        grid=(t // tm,),
        in_specs=[row(MLA_W), row(MOBA_W), row(DIL_W), full(w_out_b), row(D_MODEL), full(ln_g), full(ln_b),
                  full(rwt), full(rb)],
        out_specs=[row(D_MODEL), pl.BlockSpec((tm * TILE_ROWS, LANES), lambda i: (i, 0)),
                   pl.BlockSpec((8, tm), lambda i: (0, i)),
                   pl.BlockSpec((N_GROUPS, LANES), lambda i: (0, 0))],
        out_shape=[jax.ShapeDtypeStruct((t, D_MODEL), F32),
                   jax.ShapeDtypeStruct((t * TILE_ROWS, LANES), jnp.uint32),
                   jax.ShapeDtypeStruct((8, t), F32),
                   jax.ShapeDtypeStruct((N_GROUPS, LANES), F32)],
        compiler_params=_cparams("arbitrary"),
        name="outproj_ln_router",
    )(o_a, o_b, o_c, w_out_b, x2, ln_g, ln_b, rwt, rb)


TN_SORT = 512
TM_ROWS = 1024
DMA_UNROLL = 8


def _dest_body(grp_ref, pstart_ref, dest_ref, carry_ref):
    n = grp_ref.shape[1]

    @pl.when(pl.program_id(0) == 0)
    def _():
        carry_ref[...] = jnp.broadcast_to(pstart_ref[...], carry_ref.shape)

    before = (lax.broadcasted_iota(jnp.int32, (n, n), 0) < lax.broadcasted_iota(jnp.int32, (n, n), 1))
    before = jnp.where(before, 1.0, 0.0).astype(BF16)
    gio = lax.broadcasted_iota(jnp.int32, (N_GROUPS, n), 0)
    hit = jnp.where(gio == grp_ref[0:1, :].astype(jnp.int32), 1.0, 0.0)
    earlier = jnp.dot(hit.astype(BF16), before, preferred_element_type=F32)
    dest = jnp.sum(hit * (earlier + carry_ref[:, 0:1]), axis=0, keepdims=True)
    carry_ref[...] += jnp.sum(hit, axis=1, keepdims=True)
    dest_ref[...] = dest.astype(jnp.int32)


def _dest_rows(grp, pstart):
    t = grp.shape[1]
    return pl.pallas_call(
        _dest_body,
        grid=(t // TN_SORT,),
        in_specs=[pl.BlockSpec((8, TN_SORT), lambda i: (0, i)),
                  pl.BlockSpec((N_GROUPS, 1), lambda i: (0, 0))],
        out_specs=pl.BlockSpec((1, TN_SORT), lambda i: (0, i)),
        out_shape=jax.ShapeDtypeStruct((1, t), jnp.int32),
        scratch_shapes=[pltpu.VMEM((N_GROUPS, LANES), F32)],
        compiler_params=_cparams("arbitrary"),
        name="moe_dest",
    )(grp, pstart)


def _tile(row):
    return pl.ds(pl.multiple_of(row * TILE_ROWS, TILE_ROWS), TILE_ROWS)


def _tile_row(r, n):
    return pl.ds(r, n, stride=TILE_ROWS)


def _scatter_body(dest_ref, x_ref, xs_in_ref, xs_ref, sem):
    del xs_in_ref
    tm = x_ref.shape[0] // TILE_ROWS

    def tile_copy(r):
        return pltpu.make_async_copy(x_ref.at[_tile(r)], xs_ref.at[_tile(dest_ref[0, r])], sem)

    def issue(g, _):
        for u in range(DMA_UNROLL):
            tile_copy(g * DMA_UNROLL + u).start(priority=u % 2)
        return 0

    def drain(r, _):
        tile_copy(r).wait()
        return 0

    lax.fori_loop(0, tm // DMA_UNROLL, issue, 0)
    lax.fori_loop(0, tm, drain, 0, unroll=DMA_UNROLL)


def _scatter_rows(dest, xp, n_rows):
    t = xp.shape[0] // TILE_ROWS
    xs_init = jnp.zeros((n_rows * TILE_ROWS, LANES), xp.dtype)
    return pl.pallas_call(
        _scatter_body,
        grid=(t // TM_ROWS,),
        in_specs=[pl.BlockSpec((1, TM_ROWS), lambda i: (0, i), memory_space=pltpu.SMEM),
                  pl.BlockSpec((TM_ROWS * TILE_ROWS, LANES), lambda i: (i, 0)),
                  pl.BlockSpec(memory_space=pl.ANY)],
        out_specs=pl.BlockSpec(memory_space=pl.ANY),
        out_shape=jax.ShapeDtypeStruct(xs_init.shape, xp.dtype),
        scratch_shapes=[pltpu.SemaphoreType.DMA(())],
        input_output_aliases={2: 0},
        compiler_params=_cparams("arbitrary"),
        name="moe_scatter",
    )(dest, xp, xs_init)


def _expert_body(blk_grp_ref, nused_ref, xs_ref, w1_ref, w3_ref, w2_ref, y_ref):
    del blk_grp_ref
    i = pl.program_id(0)
    rows = xs_ref.shape[0] // TILE_ROWS

    @pl.when(i < nused_ref[0])
    def _():
        words = jnp.concatenate([xs_ref[_tile_row(r, rows), :] for r in range(PACK_ROWS)], axis=1)
        unpack = lambda bits: lax.bitcast_convert_type(bits, F32).astype(BF16)
        xb = jnp.concatenate([unpack(words & jnp.uint32(0xFFFF0000)), unpack(words << 16)], axis=1)
        gates = lax.bitcast_convert_type(xs_ref[_tile_row(PACK_ROWS, rows), :], F32)
        y = jnp.zeros((rows, D_MODEL), F32)
        for j in range(EXP_PER_GROUP):
            h1 = jnp.dot(xb, w1_ref[j], preferred_element_type=F32)
            h3 = jnp.dot(xb, w3_ref[j], preferred_element_type=F32)
            hb = (jax.nn.silu(h1) * h3).astype(BF16)
            y = y + jnp.dot(hb, w2_ref[j], preferred_element_type=F32) * gates[:, j:j + 1]
        for r in range(TILE_ROWS):
            y_ref[_tile_row(r, rows), :] = y[:, r * LANES:(r + 1) * LANES]

    @pl.when(i >= nused_ref[0])
    def _():
        y_ref[...] = jnp.zeros_like(y_ref)


def _expert_ffn(blk_grp, nused, xs, w1, w3, w2, layer):
    n_blk = xs.shape[0] // (MOE_BLOCK * TILE_ROWS)
    d = D_MODEL
    grp_w = lambda r, c: pl.BlockSpec((None, None, EXP_PER_GROUP, r, c),
                                      lambda i, bg, nu: (layer, bg[i], 0, 0, 0))
    tiles = pl.BlockSpec((MOE_BLOCK * TILE_ROWS, LANES), lambda i, bg, nu: (i, 0))
    grid_spec = pltpu.PrefetchScalarGridSpec(
        num_scalar_prefetch=2,
        grid=(n_blk,),
        in_specs=[tiles, grp_w(d, D_EXPERT), grp_w(d, D_EXPERT), grp_w(D_EXPERT, d)],
        out_specs=tiles,
    )
    return pl.pallas_call(
        _expert_body,
        grid_spec=grid_spec,
        out_shape=jax.ShapeDtypeStruct(xs.shape, F32),
        compiler_params=_cparams("arbitrary"),
        name="moe_experts",
    )(blk_grp, nused, xs, w1, w3, w2)


def _combine_body(dest_ref, dest_next_ref, x1_ref, g_ref, b_ref, ys_ref, o_ref, buf_ref, sems):
    i = pl.program_id(0)
    n_steps = pl.num_programs(0)
    tm = x1_ref.shape[0]

    def tile_copy(idx_ref, slot, r):
        return pltpu.make_async_copy(ys_ref.at[_tile(idx_ref[0, r])], buf_ref.at[slot, _tile(r)], sems.at[slot])

    def start_tiles(idx_ref, slot):
        def issue(g, _):
            for u in range(DMA_UNROLL):
                tile_copy(idx_ref, slot, g * DMA_UNROLL + u).start(priority=u % 2)
            return 0
        lax.fori_loop(0, tm // DMA_UNROLL, issue, 0)

    @pl.when(i == 0)
    def _():
        start_tiles(dest_ref, 0)

    @pl.when(i + 1 < n_steps)
    def _():
        start_tiles(dest_next_ref, (i + 1) % 2)

    slot = i % 2

    def drain(r, _):
        tile_copy(dest_ref, slot, r).wait()
        return 0

    lax.fori_loop(0, tm, drain, 0, unroll=DMA_UNROLL)
    ffn = jnp.concatenate([buf_ref[slot, _tile_row(r, tm), :] for r in range(TILE_ROWS)], axis=1)
    o_ref[...] = _layer_norm(ALPHA * x1_ref[...] + ffn, g_ref[...], b_ref[...])


def _combine_ln(dest, x1, ln_g, ln_b, ys):
    t, d = x1.shape
    full = lambda a: pl.BlockSpec(a.shape, lambda i: (0,) * a.ndim)
    n_steps = t // TM_ROWS
    return pl.pallas_call(
        _combine_body,
        grid=(n_steps,),
        in_specs=[pl.BlockSpec((1, TM_ROWS), lambda i: (0, i), memory_space=pltpu.SMEM),
                  pl.BlockSpec((1, TM_ROWS), lambda i: (0, jnp.minimum(i + 1, n_steps - 1)),
                               memory_space=pltpu.SMEM),
                  pl.BlockSpec((TM_ROWS, d), lambda i: (i, 0)), full(ln_g), full(ln_b),
                  pl.BlockSpec(memory_space=pl.ANY)],
        out_specs=pl.BlockSpec((TM_ROWS, d), lambda i: (i, 0)),
        out_shape=jax.ShapeDtypeStruct((t, d), F32),
        scratch_shapes=[pltpu.VMEM((2, TM_ROWS * TILE_ROWS, LANES), F32), pltpu.SemaphoreType.DMA((2,))],
        compiler_params=_cparams("arbitrary"),
        name="moe_combine_ln",
    )(dest, dest, x1, ln_g, ln_b, ys)


def _moe_layer(x1, xp, grp, counts, w1, w3, w2, layer, ln_g, ln_b):
    t = x1.shape[0]
    n_rows = t + N_GROUPS * MOE_BLOCK
    n_blk = n_rows // MOE_BLOCK
    cnt = counts[:, 0].astype(jnp.int32)
    padded = (cnt + MOE_BLOCK - 1) // MOE_BLOCK * MOE_BLOCK
    pend = jnp.cumsum(padded)
    pstart = (pend - padded).astype(F32).reshape(N_GROUPS, 1)
    blk_row0 = jnp.arange(n_blk, dtype=jnp.int32) * MOE_BLOCK
    blk_grp = jnp.minimum(jnp.sum((pend[None, :] <= blk_row0[:, None]).astype(jnp.int32), axis=1), N_GROUPS - 1)
    nused = (pend[-1:] // MOE_BLOCK).astype(jnp.int32)
    dest = _dest_rows(grp, pstart)
    xs = _scatter_rows(dest, xp, n_rows)
    ys = _expert_ffn(blk_grp, nused, xs, w1, w3, w2, layer)
    return _combine_ln(dest, x1, ln_g, ln_b, ys)


def _rope_tables(seq):
    def tab(dim):
        inv = ROPE_THETA ** (-jnp.arange(0, dim, 2, dtype=F32) / dim)
        ang = jnp.arange(seq, dtype=F32)[:, None] * inv[None, :]
        return jnp.cos(ang), jnp.sin(ang)

    c, s = tab(HEAD_DIM)
    cos64 = jnp.tile(jnp.concatenate([c, c], axis=1), (1, LANES // HEAD_DIM))
    sin64 = jnp.tile(jnp.concatenate([-s, s], axis=1), (1, LANES // HEAD_DIM))
    c, s = tab(MLA_ROPE)
    pad_l, pad_r = MLA_ROPE_LANE, LANES - MLA_ROPE_LANE - MLA_ROPE
    cosm = jnp.concatenate([jnp.ones((seq, pad_l), F32), c, c, jnp.ones((seq, pad_r), F32)], axis=1)
    sinm = jnp.concatenate([jnp.zeros((seq, pad_l), F32), -s, s, jnp.zeros((seq, pad_r), F32)], axis=1)
    return cos64, sin64, cosm, sinm


def _prep_mixer_weights(w_in, q_norm, w_uq, kv_norm, w_ukv):
    d = w_in.shape[0]
    lat = MLA_Q_LORA + MLA_KV_LORA
    w_in_p = jnp.concatenate([
        w_in[:, :lat], jnp.zeros((d, MLA_ROPE_LANE), F32), w_in[:, lat:lat + MLA_ROPE],
        jnp.zeros((d, LANES - MLA_ROPE_LANE - MLA_ROPE), F32), w_in[:, lat + MLA_ROPE:]], axis=1).astype(BF16)
    wq = w_uq.reshape(MLA_Q_LORA, MLA_HEADS, MLA_NOPE + MLA_ROPE)
    wuq_p = jnp.pad(wq, ((0, 0), (0, 0), (0, MLA_HEAD_PAD - MLA_NOPE - MLA_ROPE)))
    wuq_p = wuq_p.reshape(MLA_Q_LORA, MLA_HEADS * MLA_HEAD_PAD).astype(BF16)
    wkv = w_ukv.reshape(MLA_KV_LORA, MLA_HEADS, MLA_NOPE + MLA_V)
    wk = jnp.pad(wkv[:, :, :MLA_NOPE], ((0, 0), (0, 0), (0, MLA_HEAD_PAD - MLA_NOPE)))
    wukv_p = jnp.concatenate([wk.reshape(MLA_KV_LORA, MLA_HEADS * MLA_HEAD_PAD),
                              wkv[:, :, MLA_NOPE:].reshape(MLA_KV_LORA, MLA_W)], axis=1).astype(BF16)
    return w_in_p, q_norm.reshape(1, -1), wuq_p, kv_norm.reshape(1, -1), wukv_p


def _mixer_heads(x2, w_in, q_norm, w_uq, kv_norm, w_ukv, tabs, batch, seq):
    w_in_p, qn, wuq_p, kvn, wukv_p = _prep_mixer_weights(w_in, q_norm, w_uq, kv_norm, w_ukv)
    qm, km, vm, qb, kb, vb, qc, kc, vc = _inproj(x2, w_in_p, qn, kvn, wuq_p, wukv_p, tabs, seq)
    o_a = _flash_attention(qm, km, vm, MLA_HEADS, batch, seq, "mla_attn")
    o_b = _flash_attention(qb, kb, vb, MOBA_HEADS, batch, seq, "moba_attn")
    o_c = _dil_attention(qc, kc, vc, batch, seq)
    return o_a, o_b, o_c


def kernel(x, w_in, mla_q_norm, mla_w_uq, mla_kv_norm, mla_w_ukv, w_out, ln1_g, ln1_b, router_w, router_b, moe_w1, moe_w3, moe_w2, ln2_g, ln2_b):
    batch, seq, d = x.shape
    tabs = _rope_tables(seq)
    x2 = x.reshape(batch * seq, d)
    rwt = router_w.T
    rb = router_b.reshape(N_EXPERTS, 1)
    by_group = lambda w: w.astype(BF16).reshape(DEPTH, N_GROUPS, EXP_PER_GROUP, *w.shape[2:])
    w1g, w3g, w2g = by_group(moe_w1), by_group(moe_w3), by_group(moe_w2)
    for l in range(DEPTH):
        o_a, o_b, o_c = _mixer_heads(x2, w_in[l], mla_q_norm[l], mla_w_uq[l], mla_kv_norm[l], mla_w_ukv[l],
                                     tabs, batch, seq)
        x1, xp, grp, counts = _outproj_ln_router(o_a, o_b, o_c, w_out[l].astype(BF16), x2,
                                                 ln1_g[l].reshape(1, d), ln1_b[l].reshape(1, d), rwt, rb)
        x2 = _moe_layer(x1, xp, grp, counts, w1g, w3g, w2g, l,
                        ln2_g[l].reshape(1, d), ln2_b[l].reshape(1, d))
    return x2.reshape(batch, seq, d)
```
